```python
import math
import jax, jax.numpy as jnp
from jax import lax
import numpy as np

D_MODEL = 2048
BATCH = 8
SEQ = 2048
DEPTH = 2
DEC_BATCH = 32
DEC_SEQ = 4
PAST_LEN = 8192
PAGE_SIZE = 128

N_ATTN_LAYERS = (DEPTH + 1) // 2
N_SSM_LAYERS = DEPTH // 2
N_HEADS = 8
HEAD_DIM = 128
V_DIM = 2 * HEAD_DIM
QK_WIDTH = N_HEADS * 2 * HEAD_DIM
V_WIDTH = N_HEADS * V_DIM
ATTN_SCALE = HEAD_DIM ** -0.5
Q_BLOCK = 128
LAMBDA_INIT_SCALE = 0.1
SUBLN_EPS = 1e-5
SSM_GROUP_CH = 16
N_SSM_GROUPS = D_MODEL // SSM_GROUP_CH
SSM_STATE = 64
SSM_CHUNK = 128
DT_MIN = 0.001
DT_MAX = 0.1
N_EXPERT_GROUPS = 4
EXPERTS_PER_GROUP = 8
N_EXPERTS = N_EXPERT_GROUPS * EXPERTS_PER_GROUP
TOP_K = 2
EXPERT_FF = 1024
MOE_BLOCK = 128
RMS_EPS = 1e-6

kernel_name = 'hybrid_diffattn_s5_hmoe_step'


def rms_norm(x, g):
    xf = x.astype(jnp.float32)
    xf = xf * lax.rsqrt(jnp.mean(xf * xf, axis=-1, keepdims=True) + RMS_EPS)
    return (xf * g.astype(jnp.float32)).astype(x.dtype)


def split_qkv(h):
    b, l = h.shape[:2]
    q = h[..., :QK_WIDTH].reshape(b, l, N_HEADS, 2, HEAD_DIM)
    k = h[..., QK_WIDTH:2 * QK_WIDTH].reshape(b, l, N_HEADS, 2, HEAD_DIM)
    v = h[..., 2 * QK_WIDTH:].reshape(b, l, N_HEADS, V_DIM)
    return q, k, v


def diff_lambda(lq1, lk1, lq2, lk2, lam_init):
    f32 = jnp.float32
    return (jnp.exp(jnp.sum(lq1.astype(f32) * lk1.astype(f32)))
            - jnp.exp(jnp.sum(lq2.astype(f32) * lk2.astype(f32))) + lam_init)


def diff_weights(s, lam):
    p = jax.nn.softmax(s, axis=-1)
    return p[:, :, 0] - lam * p[:, :, 1]


def diff_attn_prompt(q, k, v, lam):
    b, l = q.shape[:2]
    n_blk = l // Q_BLOCK
    k_pos = jnp.arange(l)

    def block(i):
        qb = lax.dynamic_slice_in_dim(q, i * Q_BLOCK, Q_BLOCK, axis=1)
        s = jnp.einsum('bqhsd,bkhsd->bhsqk', qb, k, preferred_element_type=jnp.float32) * ATTN_SCALE
        q_pos = i * Q_BLOCK + jnp.arange(Q_BLOCK)
        s = jnp.where(k_pos[None, :] <= q_pos[:, None], s, -jnp.inf)
        w = diff_weights(s, lam)
        return jnp.einsum('bhqk,bkhe->bqhe', w.astype(v.dtype), v, preferred_element_type=jnp.float32)

    o = lax.map(block, jnp.arange(n_blk))
    return o.transpose(1, 0, 2, 3, 4).reshape(b, l, N_HEADS, V_DIM)


def diff_attn_sample(q, k_new, v_new, k_pages, v_pages, page_table, lam):
    b, s_len = q.shape[:2]
    n_past = page_table.shape[1] * PAGE_SIZE
    k_past = k_pages[page_table].reshape(b, n_past, N_HEADS, 2, HEAD_DIM)
    v_past = v_pages[page_table].reshape(b, n_past, N_HEADS, V_DIM)
    s_past = jnp.einsum('bqhsd,bkhsd->bhsqk', q, k_past, preferred_element_type=jnp.float32) * ATTN_SCALE
    s_new = jnp.einsum('bqhsd,bkhsd->bhsqk', q, k_new, preferred_element_type=jnp.float32) * ATTN_SCALE
    causal = jnp.arange(s_len)[None, :] <= jnp.arange(s_len)[:, None]
    s_new = jnp.where(causal, s_new, -jnp.inf)
    w = diff_weights(jnp.concatenate([s_past, s_new], axis=-1), lam)
    o = (jnp.einsum('bhqk,bkhe->bqhe', w[..., :n_past].astype(v_past.dtype), v_past,
                    preferred_element_type=jnp.float32)
         + jnp.einsum('bhqk,bkhe->bqhe', w[..., n_past:].astype(v_new.dtype), v_new,
                      preferred_element_type=jnp.float32))
    return o


def diff_head_out(o, subln, lam_init, w_o, dtype):
    o = o * lax.rsqrt(jnp.mean(o * o, axis=-1, keepdims=True) + SUBLN_EPS)
    o = o * subln.astype(jnp.float32) * (1.0 - lam_init)
    b, l = o.shape[:2]
    return o.reshape(b, l, V_WIDTH).astype(dtype) @ w_o


def s5_discretize(a_re, a_im, log_dt, b_re, b_im):
    f32 = jnp.float32
    a_re = a_re.astype(f32)
    a_im = a_im.astype(f32)
    b_re = b_re.astype(f32)
    b_im = b_im.astype(f32)
    dt = jnp.exp(log_dt.astype(f32))[:, None]
    mag = jnp.exp(dt * a_re)
    ab_re = mag * jnp.cos(dt * a_im)
    ab_im = mag * jnp.sin(dt * a_im)
    den = a_re * a_re + a_im * a_im
    nr = ab_re - 1.0
    fr = (nr * a_re + ab_im * a_im) / den
    fi = (ab_im * a_re - nr * a_im) / den
    bb_re = fr[..., None] * b_re - fi[..., None] * b_im
    bb_im = fr[..., None] * b_im + fi[..., None] * b_re
    return ab_re, ab_im, bb_re, bb_im


def complex_linear_combine(e1, e2):
    ar1, ai1, br1, bi1 = e1
    ar2, ai2, br2, bi2 = e2
    return (ar1 * ar2 - ai1 * ai2, ar1 * ai2 + ai1 * ar2,
            ar2 * br1 - ai2 * bi1 + br2, ar2 * bi1 + ai2 * br1 + bi2)


def s5_mixer(u, h0_re, h0_im, disc, c_re, c_im, d_skip, glu_w, glu_b):
    f32 = jnp.float32
    ab_re, ab_im, bb_re, bb_im = disc
    c_re = c_re.astype(f32)
    c_im = c_im.astype(f32)
    d_skip = d_skip.astype(f32)
    bt, l = u.shape[:2]
    chunk = SSM_CHUNK if l % SSM_CHUNK == 0 else l
    n_chunk = l // chunk
    ug = u.astype(f32).reshape(bt, n_chunk, chunk, N_SSM_GROUPS, SSM_GROUP_CH).transpose(1, 2, 0, 3, 4)

    def chunk_step(carry, u_blk):
        hr0, hi0 = carry
        br = jnp.einsum('tbgc,gpc->tbgp', u_blk, bb_re)
        bi = jnp.einsum('tbgc,gpc->tbgp', u_blk, bb_im)
        br = br.at[0].add(ab_re * hr0 - ab_im * hi0)
        bi = bi.at[0].add(ab_re * hi0 + ab_im * hr0)
        elems = (jnp.broadcast_to(ab_re, br.shape), jnp.broadcast_to(ab_im, bi.shape), br, bi)
        _, _, hr, hi = lax.associative_scan(complex_linear_combine, elems, axis=0)
        y = (jnp.einsum('tbgp,gcp->tbgc', hr, c_re) - jnp.einsum('tbgp,gcp->tbgc', hi, c_im)
             + d_skip * u_blk)
        return (hr[-1], hi[-1]), y

    (hr, hi), ys = lax.scan(chunk_step, (h0_re.astype(f32), h0_im.astype(f32)), ug)
    y = ys.transpose(2, 0, 1, 3, 4).reshape(bt, l, D_MODEL)
    z = jax.nn.gelu(y)
    zg = z @ glu_w.astype(f32) + glu_b.astype(f32)
    out = zg[..., :D_MODEL] * jax.nn.sigmoid(zg[..., D_MODEL:])
    return out.astype(u.dtype), hr, hi


def hier_moe(x, w_group, b_group, w_expert, b_expert, w13, w2):
    f32 = jnp.float32
    bsz, l, dm = x.shape
    t = bsz * l
    xt = x.reshape(t, dm)
    xf = xt.astype(f32)
    g_prob = jax.nn.softmax(xf @ w_group.astype(f32) + b_group.astype(f32), axis=-1)
    g_val, g_idx = lax.top_k(g_prob, 1)
    e_logits = (xf @ w_expert.astype(f32) + b_expert.astype(f32)).reshape(t, N_EXPERT_GROUPS, EXPERTS_PER_GROUP)
    e_sel = e_logits[jnp.arange(t), g_idx[:, 0]]
    e_prob = jax.nn.softmax(e_sel, axis=-1)
    e_val, e_idx = lax.top_k(e_prob, TOP_K)
    gates = g_val * e_val / jnp.sum(e_val, axis=-1, keepdims=True)
    expert_id = g_idx * EXPERTS_PER_GROUP + e_idx
    n_assign = t * TOP_K
    flat_e = expert_id.reshape(n_assign).astype(jnp.int32)
    flat_t = jnp.repeat(jnp.arange(t, dtype=jnp.int32), TOP_K)
    flat_g = gates.reshape(n_assign)
    counts = jnp.zeros((N_EXPERTS,), jnp.int32).at[flat_e].add(1)
    padded = ((counts + MOE_BLOCK - 1) // MOE_BLOCK) * MOE_BLOCK
    pad_end = jnp.cumsum(padded)
    pad_start = pad_end - padded
    start = jnp.cumsum(counts) - counts
    order = jnp.argsort(flat_e)
    se = flat_e[order]
    dest = pad_start[se] + (jnp.arange(n_assign, dtype=jnp.int32) - start[se])
    n_blocks = -(-n_assign // MOE_BLOCK) + N_EXPERTS
    n_rows = n_blocks * MOE_BLOCK
    row_tok = jnp.full((n_rows,), t, jnp.int32).at[dest].set(flat_t[order])
    row_gate = jnp.zeros((n_rows,), f32).at[dest].set(flat_g[order])
    block_exp = jnp.minimum(jnp.searchsorted(pad_end, jnp.arange(n_blocks, dtype=jnp.int32) * MOE_BLOCK,
                                             side='right'), N_EXPERTS - 1)
    x_pad = jnp.concatenate([xt, jnp.zeros((1, dm), xt.dtype)], axis=0)
    xs = x_pad[row_tok].reshape(n_blocks, MOE_BLOCK, dm)

    def expert_block(args):
        xb, e = args
        h = xb @ w13[e]
        h = jax.nn.silu(h[:, :EXPERT_FF]) * h[:, EXPERT_FF:]
        return h @ w2[e]

    ys = lax.map(expert_block, (xs, block_exp)).reshape(n_rows, dm)
    out = jax.ops.segment_sum(ys.astype(f32) * row_gate[:, None], row_tok, num_segments=t + 1)[:t]
    return out.reshape(bsz, l, dm).astype(x.dtype)


def setup_inputs(seed: int = 0) -> dict:
    key = jax.random.key(seed)
    keys = iter(jax.random.split(key, 48))
    f32 = jnp.float32

    def nrm(shape, scale):
        return scale * jax.random.normal(next(keys), shape, f32)

    n_pages = PAST_LEN // PAGE_SIZE
    n_used = DEC_BATCH * n_pages
    n_phys = n_used + max(1, n_used // 4)
    page_table = jax.random.permutation(next(keys), n_phys)[:n_used].reshape(DEC_BATCH, n_pages).astype(jnp.int32)
    na, ns = N_ATTN_LAYERS, N_SSM_LAYERS
    g, p, gc = N_SSM_GROUPS, SSM_STATE, SSM_GROUP_CH
    return {
        'x_prompt': nrm((BATCH, SEQ, D_MODEL), 1.0),
        'x_sample': nrm((DEC_BATCH, DEC_SEQ, D_MODEL), 1.0),
        'cache_k': nrm((na, n_phys, PAGE_SIZE, N_HEADS, 2, HEAD_DIM), 1.0),
        'cache_v': nrm((na, n_phys, PAGE_SIZE, N_HEADS, V_DIM), 1.0),
        'state_ssm_re': nrm((ns, DEC_BATCH, g, p), 0.5),
        'state_ssm_im': nrm((ns, DEC_BATCH, g, p), 0.5),
        'page_table': page_table,
        'norm_mix': 1.0 + nrm((DEPTH, D_MODEL), 0.02),
        'norm_ffn': 1.0 + nrm((DEPTH, D_MODEL), 0.02),
        'norm_final': 1.0 + nrm((D_MODEL,), 0.02),
        'attn_w_qkv': nrm((na, D_MODEL, 2 * QK_WIDTH + V_WIDTH), D_MODEL ** -0.5),
        'attn_lambda_q1': nrm((na, HEAD_DIM), LAMBDA_INIT_SCALE),
        'attn_lambda_k1': nrm((na, HEAD_DIM), LAMBDA_INIT_SCALE),
        'attn_lambda_q2': nrm((na, HEAD_DIM), LAMBDA_INIT_SCALE),
        'attn_lambda_k2': nrm((na, HEAD_DIM), LAMBDA_INIT_SCALE),
        'attn_subln': 1.0 + nrm((na, V_DIM), 0.02),
        'attn_w_o': nrm((na, V_WIDTH, D_MODEL), V_WIDTH ** -0.5),
        'ssm_A_re': -0.5 + nrm((ns, g, p), 0.01),
        'ssm_A_im': math.pi * jnp.arange(p, dtype=f32) + nrm((ns, g, p), 0.01),
        'ssm_log_dt': jax.random.uniform(next(keys), (ns, g), f32, math.log(DT_MIN), math.log(DT_MAX)),
        'ssm_B_re': nrm((ns, g, p, gc), (2 * gc) ** -0.5),
        'ssm_B_im': nrm((ns, g, p, gc), (2 * gc) ** -0.5),
        'ssm_C_re': nrm((ns, g, gc, p), p ** -0.5),
        'ssm_C_im': nrm((ns, g, gc, p), p ** -0.5),
        'ssm_D': nrm((ns, g, gc), 1.0),
        'ssm_glu_w': nrm((ns, D_MODEL, 2 * D_MODEL), D_MODEL ** -0.5),
        'ssm_glu_b': nrm((ns, 2 * D_MODEL), 0.01),
        'moe_w_group': nrm((DEPTH, D_MODEL, N_EXPERT_GROUPS), D_MODEL ** -0.5),
        'moe_b_group': nrm((DEPTH, N_EXPERT_GROUPS), 0.01),
        'moe_w_expert': nrm((DEPTH, D_MODEL, N_EXPERTS), D_MODEL ** -0.5),
        'moe_b_expert': nrm((DEPTH, N_EXPERTS), 0.01),
        'moe_w13': nrm((DEPTH, N_EXPERTS, D_MODEL, 2 * EXPERT_FF), D_MODEL ** -0.5),
        'moe_w2': nrm((DEPTH, N_EXPERTS, EXPERT_FF, D_MODEL), EXPERT_FF ** -0.5),
    }


def reference(x_prompt, x_sample, cache_k, cache_v, state_ssm_re, state_ssm_im, page_table,
              norm_mix, norm_ffn, norm_final,
              attn_w_qkv, attn_lambda_q1, attn_lambda_k1, attn_lambda_q2, attn_lambda_k2, attn_subln, attn_w_o,
              ssm_A_re, ssm_A_im, ssm_log_dt, ssm_B_re, ssm_B_im, ssm_C_re, ssm_C_im, ssm_D, ssm_glu_w, ssm_glu_b,
              moe_w_group, moe_b_group, moe_w_expert, moe_b_expert, moe_w13, moe_w2):
    xp, xs = x_prompt, x_sample
    k_p, v_p, k_s, v_s = [], [], [], []
    hr_p, hi_p, hr_s, hi_s = [], [], [], []
    for i in range(DEPTH):
        hp = rms_norm(xp, norm_mix[i])
        hs = rms_norm(xs, norm_mix[i])
        if i % 2 == 0:
            a = i // 2
            lam_init = 0.8 - 0.6 * math.exp(-0.3 * i)
            lam = diff_lambda(attn_lambda_q1[a], attn_lambda_k1[a], attn_lambda_q2[a], attn_lambda_k2[a], lam_init)
            qp, kp, vp = split_qkv(hp @ attn_w_qkv[a])
            qs, ks, vs = split_qkv(hs @ attn_w_qkv[a])
            op = diff_attn_prompt(qp, kp, vp, lam)
            os_ = diff_attn_sample(qs, ks, vs, cache_k[a], cache_v[a], page_table, lam)
            xp = xp + diff_head_out(op, attn_subln[a], lam_init, attn_w_o[a], xp.dtype)
            xs = xs + diff_head_out(os_, attn_subln[a], lam_init, attn_w_o[a], xs.dtype)
            k_p.append(kp)
            v_p.append(vp)
            k_s.append(ks)
            v_s.append(vs)
        else:
            s = i // 2
            disc = s5_discretize(ssm_A_re[s], ssm_A_im[s], ssm_log_dt[s], ssm_B_re[s], ssm_B_im[s])
            h0 = jnp.zeros((xp.shape[0], N_SSM_GROUPS, SSM_STATE), jnp.float32)
            yp, hrp, hip = s5_mixer(hp, h0, h0, disc, ssm_C_re[s], ssm_C_im[s], ssm_D[s],
                                    ssm_glu_w[s], ssm_glu_b[s])
            ys_, hrs, his = s5_mixer(hs, state_ssm_re[s], state_ssm_im[s], disc, ssm_C_re[s], ssm_C_im[s],
                                     ssm_D[s], ssm_glu_w[s], ssm_glu_b[s])
            xp = xp + yp
            xs = xs + ys_
            hr_p.append(hrp)
            hi_p.append(hip)
            hr_s.append(hrs)
            hi_s.append(his)
        xp = xp + hier_moe(rms_norm(xp, norm_ffn[i]), moe_w_group[i], moe_b_group[i], moe_w_expert[i],
                           moe_b_expert[i], moe_w13[i], moe_w2[i])
        xs = xs + hier_moe(rms_norm(xs, norm_ffn[i]), moe_w_group[i], moe_b_group[i], moe_w_expert[i],
                           moe_b_expert[i], moe_w13[i], moe_w2[i])
    y_prompt = rms_norm(xp, norm_final)
    y_sample = rms_norm(xs, norm_final)
    return (y_prompt, y_sample, jnp.stack(k_p), jnp.stack(v_p), jnp.stack(k_s), jnp.stack(v_s),
            jnp.stack(hr_p), jnp.stack(hi_p), jnp.stack(hr_s), jnp.stack(hi_s))
```

```python
import functools
import math

import jax
import jax.numpy as jnp
from jax import lax
from jax.experimental import pallas as pl
from jax.experimental.pallas import tpu as pltpu

F32 = jnp.float32
BF16 = jnp.bfloat16

RMS_EPS = 1e-6
SUBLN_EPS = 1e-5
HEAD_DIM = 128
HEAD_W = 2 * HEAD_DIM
N_HEADS = 8
PAGE_SIZE = 128
N_EXPERT_GROUPS = 4
EXPERTS_PER_GROUP = 8
N_EXPERTS = N_EXPERT_GROUPS * EXPERTS_PER_GROUP
EXPERT_FF = 1024
SSM_GROUP_CH = 16
SSM_STATE = 64
SSM_CHUNK = 16
LANES = 128
VMEM_LIMIT = 56 * 1024 * 1024


def _params(sem, vmem=VMEM_LIMIT):
    return pltpu.CompilerParams(dimension_semantics=sem, vmem_limit_bytes=vmem)


def _rms(x, g, eps):
    return x * lax.rsqrt(jnp.mean(x * x, axis=-1, keepdims=True) + eps) * g


def _split3(x):
    hi = x.astype(BF16)
    lo = (x - hi.astype(F32)).astype(BF16)
    return hi, lo


def _dot3(a, b, dims):
    ah, al = _split3(a)
    bh, bl = _split3(b)
    dn = (dims, ((), ()))
    d = functools.partial(lax.dot_general, dimension_numbers=dn, preferred_element_type=F32)
    return d(ah, bh) + d(al, bh) + d(ah, bl)


_NT = ((1,), (1,))
_NN = ((1,), (0,))


def _qkv_kernel(x_ref, g_ref, w_ref, q_ref, k_ref, v_ref, xn_ref, *, nj):
    j = pl.program_id(1)

    @pl.when(j == 0)
    def _():
        xn_ref[...] = _rms(x_ref[...], g_ref[...], RMS_EPS).astype(BF16)

    acc = jnp.dot(xn_ref[...], w_ref[...], preferred_element_type=F32)
    for t, o_ref in enumerate((q_ref, k_ref, v_ref)):
        @pl.when((j >= t * nj) & (j < (t + 1) * nj))
        def _(o_ref=o_ref):
            o_ref[...] = acc


def qkv_proj(x, g, w_bf, tm, tn):
    t, d = x.shape
    nj = d // tn

    def omap(part):
        return lambda i, j: (i, jnp.clip(j - part * nj, 0, nj - 1))

    return pl.pallas_call(
        functools.partial(_qkv_kernel, nj=nj),
        grid=(t // tm, 3 * nj),
        in_specs=[pl.BlockSpec((tm, d), lambda i, j: (i, 0)),
                  pl.BlockSpec((1, d), lambda i, j: (0, 0)),
                  pl.BlockSpec((d, tn), lambda i, j: (0, j))],
        out_specs=[pl.BlockSpec((tm, tn), omap(p)) for p in range(3)],
        out_shape=[jax.ShapeDtypeStruct((t, d), F32)] * 3,
        scratch_shapes=[pltpu.VMEM((tm, d), BF16)],
        compiler_params=_params(("arbitrary", "arbitrary")),
        name="qkv_proj",
    )(x, g, w_bf)


def _mm_res_kernel(a_ref, w_ref, x_ref, o_ref):
    o_ref[...] = x_ref[...] + jnp.dot(a_ref[...].astype(BF16), w_ref[...], preferred_element_type=F32)


def matmul_residual(a, w_bf, x, tm, tn):
    t, kd = a.shape
    n = w_bf.shape[1]
    return pl.pallas_call(
        _mm_res_kernel,
        grid=(t // tm, n // tn),
        in_specs=[pl.BlockSpec((tm, kd), lambda i, j: (i, 0)),
                  pl.BlockSpec((kd, tn), lambda i, j: (0, j)),
                  pl.BlockSpec((tm, tn), lambda i, j: (i, j))],
        out_specs=pl.BlockSpec((tm, tn), lambda i, j: (i, j)),
        out_shape=jax.ShapeDtypeStruct((t, n), F32),
        compiler_params=_params(("arbitrary", "arbitrary")),
        name="matmul_residual",
    )(a, w_bf, x)


def _glu_kernel(z_ref, w1_ref, w2_ref, b1_ref, b2_ref, x_ref, o_ref):
    z = z_ref[...]
    a = jnp.dot(z, w1_ref[...], preferred_element_type=F32) + b1_ref[...]
    b = jnp.dot(z, w2_ref[...], preferred_element_type=F32) + b2_ref[...]
    o_ref[...] = x_ref[...] + a * (1.0 / (1.0 + jnp.exp(-b)))


def glu_residual(z, w_bf, bias, x, tm, tn):
    t, d = z.shape
    nj = d // tn
    return pl.pallas_call(
        _glu_kernel,
        grid=(t // tm, nj),
        in_specs=[pl.BlockSpec((tm, d), lambda i, j: (i, 0)),
                  pl.BlockSpec((d, tn), lambda i, j: (0, j)),
                  pl.BlockSpec((d, tn), lambda i, j: (0, j + nj)),
                  pl.BlockSpec((1, tn), lambda i, j: (0, j)),
                  pl.BlockSpec((1, tn), lambda i, j: (0, j + nj)),
                  pl.BlockSpec((tm, tn), lambda i, j: (i, j))],
        out_specs=pl.BlockSpec((tm, tn), lambda i, j: (i, j)),
        out_shape=jax.ShapeDtypeStruct((t, d), F32),
        compiler_params=_params(("arbitrary", "arbitrary")),
        name="glu_residual",
    )(z, w_bf, w_bf, bias, bias, x)


def _norm_kernel(x_ref, g_ref, o_ref):
    o_ref[...] = _rms(x_ref[...], g_ref[...], RMS_EPS).astype(o_ref.dtype)


def rms_norm(x, g, tm):
    t, d = x.shape
    return pl.pallas_call(
        _norm_kernel,
        grid=(t // tm,),
        in_specs=[pl.BlockSpec((tm, d), lambda i: (i, 0)),
                  pl.BlockSpec((1, d), lambda i: (0, 0))],
        out_specs=pl.BlockSpec((tm, d), lambda i: (i, 0)),
        out_shape=jax.ShapeDtypeStruct((t, d), F32),
        compiler_params=_params(("arbitrary",)),
        name="rms_norm",
    )(x, g)


def _diff_lambda(lq1, lk1, lq2, lk2, lam_init):
    return (jnp.exp(jnp.sum(lq1[...] * lk1[...], axis=-1, keepdims=True))
            - jnp.exp(jnp.sum(lq2[...] * lk2[...], axis=-1, keepdims=True)) + lam_init)


def _head_out(acc, l, lam, subln, lam_init, n):
    o = acc[:n] / l[:n] - lam * (acc[n:] / l[n:])
    o = o * lax.rsqrt(jnp.mean(o * o, axis=-1, keepdims=True) + SUBLN_EPS)
    return o * subln * (1.0 - lam_init)


def _attn_prompt_kernel(q_ref, k_ref, v_ref, lq1, lk1, lq2, lk2, subln_ref, o_ref,
                        kbf, vbf, m_sc, l_sc, acc_sc, *, tq, lam_init):
    seq = q_ref.shape[0]
    scale = HEAD_DIM ** -0.5
    kbf[...] = k_ref[...].astype(BF16)
    vbf[...] = v_ref[...].astype(BF16)
    lam = _diff_lambda(lq1, lk1, lq2, lk2, lam_init)
    row = lax.broadcasted_iota(jnp.int32, (2 * tq, tq), 0)
    col = lax.broadcasted_iota(jnp.int32, (2 * tq, tq), 1)
    causal = col <= jnp.where(row >= tq, row - tq, row)

    def q_body(qi, carry):
        q0 = pl.multiple_of(qi * tq, tq)
        qt = q_ref[pl.ds(q0, tq), :].astype(BF16)
        qa, qb = qt[:, :HEAD_DIM], qt[:, HEAD_DIM:]
        m_sc[...] = jnp.full(m_sc.shape, -jnp.inf, F32)
        l_sc[...] = jnp.zeros(l_sc.shape, F32)
        acc_sc[...] = jnp.zeros(acc_sc.shape, F32)

        def kv_step(kj, masked):
            k0 = pl.multiple_of(kj * tq, tq)
            kt = kbf[pl.ds(k0, tq), :]
            vt = vbf[pl.ds(k0, tq), :]
            s0 = lax.dot_general(qa, kt[:, :HEAD_DIM], (_NT, ((), ())), preferred_element_type=F32)
            s1 = lax.dot_general(qb, kt[:, HEAD_DIM:], (_NT, ((), ())), preferred_element_type=F32)
            s = jnp.concatenate([s0, s1], axis=0) * scale
            if masked:
                s = jnp.where(causal, s, -jnp.inf)
            m_prev = m_sc[...]
            m_new = jnp.maximum(m_prev, jnp.max(s, axis=-1, keepdims=True))
            alpha = jnp.exp(m_prev - m_new)
            p = jnp.exp(s - m_new)
            l_sc[...] = alpha * l_sc[...] + jnp.sum(p, axis=-1, keepdims=True)
            acc_sc[...] = alpha * acc_sc[...] + jnp.dot(p.astype(BF16), vt, preferred_element_type=F32)
            m_sc[...] = m_new

        def kv_body(kj, c):
            kv_step(kj, False)
            return c

        lax.fori_loop(0, qi, kv_body, 0)
        kv_step(qi, True)
        o = _head_out(acc_sc[...], l_sc[...], lam, subln_ref[...], lam_init, tq)
        o_ref[pl.ds(q0, tq), :] = o.astype(o_ref.dtype)
        return carry

    lax.fori_loop(0, seq // tq, q_body, 0)


def attn_prompt(q, k, v, lams, subln, batch, seq, lam_init, tq=256):
    t, d = q.shape
    blk = pl.BlockSpec((seq, HEAD_W), lambda b, h: (b, h))
    vec = pl.BlockSpec((1, HEAD_DIM), lambda b, h: (0, 0))
    return pl.pallas_call(
        functools.partial(_attn_prompt_kernel, tq=tq, lam_init=lam_init),
        grid=(batch, N_HEADS),
        in_specs=[blk, blk, blk, vec, vec, vec, vec,
                  pl.BlockSpec((1, HEAD_W), lambda b, h: (0, 0))],
        out_specs=blk,
        out_shape=jax.ShapeDtypeStruct((t, d), BF16),
        scratch_shapes=[pltpu.VMEM((seq, HEAD_W), BF16), pltpu.VMEM((seq, HEAD_W), BF16),
                        pltpu.VMEM((2 * tq, 1), F32), pltpu.VMEM((2 * tq, 1), F32),
                        pltpu.VMEM((2 * tq, HEAD_W), F32)],
        compiler_params=_params(("arbitrary", "arbitrary")),
        name="attn_prompt",
    )(q, k, v, *lams, subln)


def _attn_sample_kernel(pt_ref, q_ref, kn_ref, vn_ref, kp_ref, vp_ref, lq1, lk1, lq2, lk2, subln_ref,
                        o_ref, qblk, m_sc, l_sc, acc_sc, *, n_new, lam_init):
    del pt_ref
    p = pl.program_id(1)
    scale = HEAD_DIM ** -0.5
    nq = 2 * n_new

    @pl.when(p == 0)
    def _():
        row = lax.broadcasted_iota(jnp.int32, qblk.shape, 0)
        lane = lax.broadcasted_iota(jnp.int32, qblk.shape, 1)
        first_map = (lane % HEAD_W) < HEAD_DIM
        qblk[...] = jnp.where((row < n_new) == first_map, q_ref[...], 0.0)
        m_sc[...] = jnp.full(m_sc.shape, -jnp.inf, F32)
        l_sc[...] = jnp.zeros(l_sc.shape, F32)
        acc_sc[...] = jnp.zeros(acc_sc.shape, F32)

    for h in range(N_HEADS):
        hs = slice(h * HEAD_W, (h + 1) * HEAD_W)
        kh = kp_ref[:, hs].astype(BF16)
        vh = vp_ref[:, hs].astype(BF16)
        s = lax.dot_general(qblk[:, hs].astype(BF16), kh, (_NT, ((), ())), preferred_element_type=F32) * scale
        m_prev = m_sc[h]
        m_new = jnp.maximum(m_prev, jnp.max(s, axis=-1, keepdims=True))
        alpha = jnp.exp(m_prev - m_new)
        pr = jnp.exp(s - m_new)
        l_sc[h] = alpha * l_sc[h] + jnp.sum(pr, axis=-1, keepdims=True)
        acc_sc[h] = alpha * acc_sc[h] + jnp.dot(pr.astype(BF16), vh, preferred_element_type=F32)
        m_sc[h] = m_new

    @pl.when(p == pl.num_programs(1) - 1)
    def _():
        lam = _diff_lambda(lq1, lk1, lq2, lk2, lam_init)
        qpos = lax.broadcasted_iota(jnp.int32, (nq, 1), 0) % n_new
        for h in range(N_HEADS):
            hs = slice(h * HEAD_W, (h + 1) * HEAD_W)
            qh = qblk[:, hs]
            m, l, acc = m_sc[h], l_sc[h], acc_sc[h]
            for j in range(n_new):
                s = jnp.sum(qh * kn_ref[j:j + 1, hs], axis=-1, keepdims=True) * scale
                s = jnp.where(qpos >= j, s, -jnp.inf)
                m_new = jnp.maximum(m, s)
                alpha = jnp.exp(m - m_new)
                pr = jnp.exp(s - m_new)
                l = alpha * l + pr
                acc = alpha * acc + pr * vn_ref[j:j + 1, hs]
                m = m_new
            o_ref[:, hs] = _head_out(acc, l, lam, subln_ref[...], lam_init, n_new)


def attn_sample(q, k_new, v_new, cache_k, cache_v, page_table, lams, subln, layer, lam_init):
    b, n_new, d = q.shape
    n_pages = page_table.shape[1]
    q2 = jnp.concatenate([q, q], axis=1)
    tok = pl.BlockSpec((None, n_new, d), lambda i, p, pt: (i, 0, 0))
    page = pl.BlockSpec((None, None, PAGE_SIZE, d), lambda i, p, pt: (layer, pt[i * n_pages + p], 0, 0))
    vec = pl.BlockSpec((1, HEAD_DIM), lambda i, p, pt: (0, 0))
    grid_spec = pltpu.PrefetchScalarGridSpec(
        num_scalar_prefetch=1,
        grid=(b, n_pages),
        in_specs=[pl.BlockSpec((None, 2 * n_new, d), lambda i, p, pt: (i, 0, 0)), tok, tok, page, page,
                  vec, vec, vec, vec, pl.BlockSpec((1, HEAD_W), lambda i, p, pt: (0, 0))],
        out_specs=tok,
        scratch_shapes=[pltpu.VMEM((2 * n_new, d), F32),
                        pltpu.VMEM((N_HEADS, 2 * n_new, 1), F32), pltpu.VMEM((N_HEADS, 2 * n_new, 1), F32),
                        pltpu.VMEM((N_HEADS, 2 * n_new, HEAD_W), F32)])
    return pl.pallas_call(
        functools.partial(_attn_sample_kernel, n_new=n_new, lam_init=lam_init),
        grid_spec=grid_spec,
        out_shape=jax.ShapeDtypeStruct((b, n_new, d), F32),
        compiler_params=_params(("arbitrary", "arbitrary")),
        name="attn_sample",
    )(page_table.reshape(-1), q2, k_new, v_new, cache_k, cache_v, *lams, subln)


def _router_kernel(x_ref, g_ref, wh_ref, wl_ref, b_ref, xn_ref, ids_ref, gates_ref):
    xn = _rms(x_ref[...], g_ref[...], RMS_EPS)
    xn_ref[...] = xn
    xh, xl = _split3(xn)
    d = functools.partial(jnp.dot, preferred_element_type=F32)
    logits = d(xh, wh_ref[...]) + d(xl, wh_ref[...]) + d(xh, wl_ref[...]) + b_ref[...]
    lane = lax.broadcasted_iota(jnp.int32, logits.shape, 1)
    ninf = -jnp.inf
    gl = jnp.where(lane < N_EXPERT_GROUPS, logits, ninf)
    gmax = jnp.max(gl, axis=-1, keepdims=True)
    g_val = 1.0 / jnp.sum(jnp.exp(gl - gmax), axis=-1, keepdims=True)
    g_idx = jnp.min(jnp.where(gl == gmax, lane, LANES), axis=-1, keepdims=True)
    lo = N_EXPERT_GROUPS + EXPERTS_PER_GROUP * g_idx
    el = jnp.where((lane >= lo) & (lane < lo + EXPERTS_PER_GROUP), logits, ninf)
    e1 = jnp.max(el, axis=-1, keepdims=True)
    i1 = jnp.min(jnp.where(el == e1, lane, LANES), axis=-1, keepdims=True)
    el2 = jnp.where(lane == i1, ninf, el)
    e2 = jnp.max(el2, axis=-1, keepdims=True)
    i2 = jnp.min(jnp.where(el2 == e2, lane, LANES), axis=-1, keepdims=True)
    r = jnp.exp(e2 - e1)
    w1 = g_val / (1.0 + r)
    w2 = g_val * r / (1.0 + r)
    ids_ref[...] = jnp.where(lane == 0, i1 - N_EXPERT_GROUPS, jnp.where(lane == 1, i2 - N_EXPERT_GROUPS, 0))
    gates_ref[...] = jnp.where(lane == 0, w1, jnp.where(lane == 1, w2, 0.0))


def moe_route(x, g, w_hi, w_lo, bias, tm):
    t, d = x.shape
    row = lambda i: (i, 0)
    const = lambda i: (0, 0)
    return pl.pallas_call(
        _router_kernel,
        grid=(t // tm,),
        in_specs=[pl.BlockSpec((tm, d), row), pl.BlockSpec((1, d), const),
                  pl.BlockSpec((d, LANES), const), pl.BlockSpec((d, LANES), const),
                  pl.BlockSpec((1, LANES), const)],
        out_specs=[pl.BlockSpec((tm, d), row), pl.BlockSpec((tm, LANES), row), pl.BlockSpec((tm, LANES), row)],
        out_shape=[jax.ShapeDtypeStruct((t, d), F32), jax.ShapeDtypeStruct((t, LANES), jnp.int32),
                   jax.ShapeDtypeStruct((t, LANES), F32)],
        compiler_params=_params(("arbitrary",)),
        name="moe_route",
    )(x, g, w_hi, w_lo, bias)


def _row_gather_start(idx_ref, base, n, src_hbm, dst, sem):
    def body(r, c):
        pltpu.make_async_copy(src_hbm.at[pl.ds(idx_ref[base + r], 1), :], dst.at[pl.ds(r, 1), :], sem).start()
        return c
    lax.fori_loop(0, n, body, 0, unroll=8)


def _row_gather_wait(dst, sem):
    pltpu.make_async_copy(dst, dst, sem).wait()


def _expert_kernel(bexp_ref, rtok_ref, nused_ref, xn_hbm, w13_ref, w2_ref, ys_ref, xbuf, sem, *, tm):
    del bexp_ref
    rb = pl.program_id(0)
    nused = nused_ref[0]

    @pl.when(rb == 0)
    def _():
        _row_gather_start(rtok_ref, 0, tm, xn_hbm, xbuf.at[0], sem.at[0])

    @pl.when(rb + 1 < nused)
    def _():
        nxt = (rb + 1) % 2
        _row_gather_start(rtok_ref, (rb + 1) * tm, tm, xn_hbm, xbuf.at[nxt], sem.at[nxt])

    @pl.when(rb < nused)
    def _():
        slot = rb % 2
        _row_gather_wait(xbuf.at[slot], sem.at[slot])
        x = xbuf[slot].astype(BF16)
        h = jnp.dot(x, w13_ref[...], preferred_element_type=F32)
        ga, up = h[:, :EXPERT_FF], h[:, EXPERT_FF:]
        a = ga * (1.0 / (1.0 + jnp.exp(-ga))) * up
        ys_ref[...] = jnp.dot(a.astype(BF16), w2_ref[...], preferred_element_type=F32)

    @pl.when(rb >= nused)
    def _():
        ys_ref[...] = jnp.zeros(ys_ref.shape, F32)


def moe_experts(xn, w13_bf, w2_bf, block_exp, row_tok, n_used, tm):
    t, d = xn.shape
    n_blocks = block_exp.shape[0]
    grid_spec = pltpu.PrefetchScalarGridSpec(
        num_scalar_prefetch=3,
        grid=(n_blocks,),
        in_specs=[pl.BlockSpec(memory_space=pl.ANY),
                  pl.BlockSpec((None, d, 2 * EXPERT_FF), lambda rb, be, rt, nu: (be[rb], 0, 0)),
                  pl.BlockSpec((None, EXPERT_FF, d), lambda rb, be, rt, nu: (be[rb], 0, 0))],
        out_specs=pl.BlockSpec((tm, d), lambda rb, be, rt, nu: (rb, 0)),
        scratch_shapes=[pltpu.VMEM((2, tm, d), F32), pltpu.SemaphoreType.DMA((2,))])
    return pl.pallas_call(
        functools.partial(_expert_kernel, tm=tm),
        grid_spec=grid_spec,
        out_shape=jax.ShapeDtypeStruct((n_blocks * tm, d), F32),
        compiler_params=_params(("arbitrary",)),
        name="moe_experts",
    )(block_exp, row_tok, n_used, xn, w13_bf, w2_bf)


def _combine_kernel(pos_ref, ys_hbm, x_ref, gates_ref, o_ref, ybuf, sem, *, tm):
    i = pl.program_id(0)
    n = pl.num_programs(0)

    def start(tile, slot):
        _row_gather_start(pos_ref, tile * 2 * tm, 2 * tm, ys_hbm, ybuf.at[slot], sem.at[slot])

    @pl.when(i == 0)
    def _():
        start(0, 0)

    @pl.when(i + 1 < n)
    def _():
        start(i + 1, (i + 1) % 2)

    slot = i % 2
    _row_gather_wait(ybuf.at[slot], sem.at[slot])
    g = gates_ref[...]
    o_ref[...] = x_ref[...] + (g[:, 0:1] * ybuf[slot, :tm] + g[:, 1:2] * ybuf[slot, tm:])


def moe_combine(ys, x, gates, pos, tm):
    t, d = x.shape
    pos_tiles = pos.reshape(t // tm, tm, 2).transpose(0, 2, 1).reshape(-1)
    grid_spec = pltpu.PrefetchScalarGridSpec(
        num_scalar_prefetch=1,
        grid=(t // tm,),
        in_specs=[pl.BlockSpec(memory_space=pl.ANY),
                  pl.BlockSpec((tm, d), lambda i, ps: (i, 0)),
                  pl.BlockSpec((tm, LANES), lambda i, ps: (i, 0))],
        out_specs=pl.BlockSpec((tm, d), lambda i, ps: (i, 0)),
        scratch_shapes=[pltpu.VMEM((2, 2 * tm, d), F32), pltpu.SemaphoreType.DMA((2,))])
    return pl.pallas_call(
        functools.partial(_combine_kernel, tm=tm),
        grid_spec=grid_spec,
        out_shape=jax.ShapeDtypeStruct((t, d), F32),
        compiler_params=_params(("arbitrary",)),
        name="moe_combine",
    )(pos_tiles, ys, x, gates)


def hier_moe(xs, g, w_group, b_group, w_expert, b_expert, w13_bf, w2_bf, tm_blk=256):
    d = xs[0].shape[1]
    pad = LANES - N_EXPERT_GROUPS - N_EXPERTS
    w_cat = jnp.concatenate([w_group, w_expert, jnp.zeros((d, pad), F32)], axis=1)
    b_cat = jnp.concatenate([b_group, b_expert, jnp.zeros((pad,), F32)])[None]
    w_hi = w_cat.astype(BF16)
    w_lo = (w_cat - w_hi.astype(F32)).astype(BF16)
    routed = [moe_route(x, g, w_hi, w_lo, b_cat, min(512, x.shape[0])) for x in xs]
    xn = jnp.concatenate([r[0] for r in routed], axis=0)
    ids = jnp.concatenate([r[1][:, :2] for r in routed], axis=0)
    t = xn.shape[0]
    n_assign = 2 * t
    flat_e = ids.reshape(n_assign)
    onehot = (flat_e[:, None] == jnp.arange(N_EXPERTS, dtype=jnp.int32)[None, :]).astype(jnp.int32)
    csum = jnp.cumsum(onehot, axis=0)
    counts = csum[-1]
    rank = jnp.take_along_axis(csum, flat_e[:, None], axis=1)[:, 0] - 1
    padded = ((counts + tm_blk - 1) // tm_blk) * tm_blk
    pad_end = jnp.cumsum(padded)
    pad_start = pad_end - padded
    dest = pad_start[flat_e] + rank
    n_blocks = -(-n_assign // tm_blk) + N_EXPERTS
    flat_t = jnp.arange(n_assign, dtype=jnp.int32) // 2
    row_tok = jnp.zeros((n_blocks * tm_blk,), jnp.int32).at[dest].set(flat_t)
    block_exp = jnp.minimum(jnp.searchsorted(pad_end, jnp.arange(n_blocks, dtype=jnp.int32) * tm_blk, side='right'),
                            N_EXPERTS - 1).astype(jnp.int32)
    n_used = (pad_end[-1:] // tm_blk).astype(jnp.int32)
    ys = moe_experts(xn, w13_bf, w2_bf, block_exp, row_tok, n_used, tm_blk)
    pos = dest.reshape(t, 2)
    outs, off = [], 0
    for x, r in zip(xs, routed):
        n = x.shape[0]
        outs.append(moe_combine(ys, x, r[2], pos[off:off + n], min(128, n)))
        off += n
    return outs


def _cmul(ar, ai, br, bi):
    return ar * br - ai * bi, ar * bi + ai * br


def _ssm_prep_kernel(are_ref, aim_ref, ldt_ref, bre_ref, bim_ref, cre_ref, cim_ref,
                     m_ref, bc_ref, cc_ref, a16_ref, a4_ref, *, n_new):
    p = SSM_STATE
    nc = SSM_CHUNK
    a_re, a_im = are_ref[...], aim_ref[...]
    dt = jnp.exp(ldt_ref[...])
    mag = jnp.exp(dt * a_re)
    ab_re = mag * jnp.cos(dt * a_im)
    ab_im = mag * jnp.sin(dt * a_im)
    den = a_re * a_re + a_im * a_im
    nr = ab_re - 1.0
    fr = (nr * a_re + ab_im * a_im) / den
    fi = (ab_im * a_re - nr * a_im) / den
    bt_re, bt_im = bre_ref[...], bim_ref[...]
    bb_re = fr * bt_re - fi * bt_im
    bb_im = fr * bt_im + fi * bt_re
    c_re, c_im = cre_ref[...], cim_ref[...]
    pw = [(jnp.ones_like(ab_re), jnp.zeros_like(ab_re))]
    for _ in range(nc):
        pw.append(_cmul(pw[-1][0], pw[-1][1], ab_re, ab_im))
    wst = []
    for tau in range(nc + 1):
        wr, wi = _cmul(c_re, c_im, pw[tau][0], pw[tau][1])
        wst.append(jnp.concatenate([wr, -wi], axis=1))
    wstack = jnp.concatenate(wst[:nc], axis=0)
    bs_t = jnp.concatenate([bb_re, bb_im], axis=1)
    q = _dot3(bs_t, wstack, _NT)
    qpad = jnp.concatenate([jnp.zeros_like(q), q], axis=1)
    w = nc * SSM_GROUP_CH
    rows = []
    for s in range(nc):
        off = (nc - s) * SSM_GROUP_CH
        rows.append(qpad[:, off:off + w])
    m_ref[...] = jnp.concatenate(rows, axis=0).astype(m_ref.dtype)
    bc = []
    for s in range(nc):
        xr, xi = _cmul(bb_re, bb_im, pw[nc - 1 - s][0], pw[nc - 1 - s][1])
        bc.append(jnp.concatenate([xr, xi], axis=1))
    bc_ref[...] = jnp.concatenate(bc, axis=0).astype(bc_ref.dtype)
    cc_ref[...] = jnp.concatenate(wst[1:], axis=0).astype(cc_ref.dtype)
    a16_ref[...] = jnp.concatenate([pw[nc][0], pw[nc][1]], axis=1)
    a4_ref[...] = jnp.concatenate([pw[n_new][0], pw[n_new][1]], axis=1)


def ssm_prep(a_re, a_im, log_dt, b_re, b_im, c_re, c_im, n_new):
    g, p = a_re.shape
    gc = SSM_GROUP_CH
    w = SSM_CHUNK * gc
    vec = pl.BlockSpec((None, 1, p), lambda i: (i, 0, 0))
    mat = pl.BlockSpec((None, gc, p), lambda i: (i, 0, 0))
    bt_re = b_re.transpose(0, 2, 1)
    bt_im = b_im.transpose(0, 2, 1)
    return pl.pallas_call(
        functools.partial(_ssm_prep_kernel, n_new=n_new),
        grid=(g,),
        in_specs=[vec, vec, pl.BlockSpec((None, 1, 1), lambda i: (i, 0, 0)), mat, mat, mat, mat],
        out_specs=[pl.BlockSpec((None, w, w), lambda i: (i, 0, 0)),
                   pl.BlockSpec((None, w, 2 * p), lambda i: (i, 0, 0)),
                   pl.BlockSpec((None, w, 2 * p), lambda i: (i, 0, 0)),
                   pl.BlockSpec((None, 1, 2 * p), lambda i: (i, 0, 0)),
                   pl.BlockSpec((None, 1, 2 * p), lambda i: (i, 0, 0))],
        out_shape=[jax.ShapeDtypeStruct((g, w, w), BF16),
                   jax.ShapeDtypeStruct((g, w, 2 * p), BF16),
                   jax.ShapeDtypeStruct((g, w, 2 * p), BF16),
                   jax.ShapeDtypeStruct((g, 1, 2 * p), F32),
                   jax.ShapeDtypeStruct((g, 1, 2 * p), F32)],
        compiler_params=_params(("arbitrary",)),
        name="ssm_prep",
    )(a_re[:, None], a_im[:, None], log_dt[:, None, None], bt_re, bt_im, c_re, c_im)


def _ssm_x_kernel(u_ref, bc_ref, x_ref, *, n_prompt, n_new):
    u = u_ref[...].astype(BF16)
    bc = bc_ref[...]
    x_ref[:n_prompt] = jnp.dot(u[:n_prompt], bc, preferred_element_type=F32)
    k = n_new * SSM_GROUP_CH
    x_ref[n_prompt:] = jnp.dot(u[n_prompt:, :k], bc[bc.shape[0] - k:], preferred_element_type=F32)


def _ssm_scan_kernel(x_ref, h0_ref, a16_ref, a4_ref, hs_ref, hfin_ref, hsfin_ref, *, n_chunks, batch):
    p = SSM_STATE

    def coef(a):
        ar, ai = a[:, :, :p], a[:, :, p:]
        return jnp.concatenate([ar, ar], axis=-1), jnp.concatenate([-ai, ai], axis=-1)

    def step(h, c1, c2):
        return c1 * h + c2 * pltpu.roll(h, p, 2)

    c1, c2 = coef(a16_ref[...])
    n_prompt = n_chunks * batch

    def body(c, h):
        r0 = pl.multiple_of(c * batch, batch)
        hs_ref[:, pl.ds(r0, batch), :] = h
        return step(h, c1, c2) + x_ref[:, pl.ds(r0, batch), :]

    h0 = jnp.zeros((x_ref.shape[0], batch, 2 * p), F32)
    hfin_ref[...] = lax.fori_loop(0, n_chunks, body, h0)
    d1, d2 = coef(a4_ref[...])
    hs0 = h0_ref[...]
    hs_ref[:, n_prompt:, :] = hs0
    hsfin_ref[...] = step(hs0, d1, d2) + x_ref[:, n_prompt:, :]


def _ssm_y_kernel(u_ref, hs_ref, m_ref, cc_ref, d_ref, z_ref):
    u = u_ref[...]
    y = jnp.dot(u.astype(BF16), m_ref[...], preferred_element_type=F32)
    y = y + lax.dot_general(hs_ref[...].astype(BF16), cc_ref[...], (_NT, ((), ())), preferred_element_type=F32)
    y = y + d_ref[...] * u
    c = math.sqrt(2.0 / math.pi)
    z = 0.5 * y * (1.0 + jnp.tanh(c * (y + 0.044715 * (y * y * y))))
    z_ref[...] = z.astype(z_ref.dtype)


def ssm_mix(u_g, h0_g, ops, d_tile, n_chunks, batch, n_new, g_blk=16):
    m_g, bc_g, cc_g, a16, a4 = ops
    g, r, w = u_g.shape
    p2 = 2 * SSM_STATE
    n_prompt = n_chunks * batch
    n_s = r - n_prompt
    per_g = lambda shape: pl.BlockSpec((None,) + shape, lambda i: (i,) + (0,) * len(shape))
    x = pl.pallas_call(
        functools.partial(_ssm_x_kernel, n_prompt=n_prompt, n_new=n_new),
        grid=(g,),
        in_specs=[per_g((r, w)), per_g((w, p2))],
        out_specs=per_g((r, p2)),
        out_shape=jax.ShapeDtypeStruct((g, r, p2), F32),
        compiler_params=_params(("arbitrary",)),
        name="ssm_chunk_state",
    )(u_g, bc_g)
    blk = lambda rows: pl.BlockSpec((g_blk, rows, p2), lambda i: (i, 0, 0))
    hs, hfin, hsfin = pl.pallas_call(
        functools.partial(_ssm_scan_kernel, n_chunks=n_chunks, batch=batch),
        grid=(g // g_blk,),
        in_specs=[blk(r), blk(n_s), blk(1), blk(1)],
        out_specs=[blk(r), blk(batch), blk(n_s)],
        out_shape=[jax.ShapeDtypeStruct((g, r, p2), F32),
                   jax.ShapeDtypeStruct((g, batch, p2), F32),
                   jax.ShapeDtypeStruct((g, n_s, p2), F32)],
        compiler_params=_params(("arbitrary",)),
        name="ssm_scan",
    )(x, h0_g, a16, a4)
    z = pl.pallas_call(
        _ssm_y_kernel,
        grid=(g,),
        in_specs=[per_g((r, w)), per_g((r, p2)), per_g((w, w)), per_g((w, p2)), per_g((1, w))],
        out_specs=per_g((r, w)),
        out_shape=jax.ShapeDtypeStruct((g, r, w), BF16),
        compiler_params=_params(("arbitrary",)),
        name="ssm_output",
    )(u_g, hs, m_g, cc_g, d_tile)
    return z, hfin, hsfin


def kernel(x_prompt, x_sample, cache_k, cache_v, state_ssm_re, state_ssm_im, page_table, norm_mix, norm_ffn, norm_final, attn_w_qkv, attn_lambda_q1, attn_lambda_k1, attn_lambda_q2, attn_lambda_k2, attn_subln, attn_w_o, ssm_A_re, ssm_A_im, ssm_log_dt, ssm_B_re, ssm_B_im, ssm_C_re, ssm_C_im, ssm_D, ssm_glu_w, ssm_glu_b, moe_w_group, moe_b_group, moe_w_expert, moe_b_expert, moe_w13, moe_w2):
    batch, seq, d = x_prompt.shape
    dec_b, dec_s, _ = x_sample.shape
    depth = norm_mix.shape[0]
    tp, ts = batch * seq, dec_b * dec_s
    xp = x_prompt.reshape(tp, d)
    xs = x_sample.reshape(ts, d)
    n_phys = cache_k.shape[1]
    g, pst, gc = ssm_B_re.shape[1:]
    n_chunks = seq // SSM_CHUNK
    k_p, v_p, k_s, v_s = [], [], [], []
    hr_p, hi_p, hr_s, hi_s = [], [], [], []
    for i in range(depth):
        gm = norm_mix[i][None]
        if i % 2 == 0:
            a = i // 2
            lam_init = 0.8 - 0.6 * math.exp(-0.3 * i)
            lams = (attn_lambda_q1[a][None], attn_lambda_k1[a][None], attn_lambda_q2[a][None], attn_lambda_k2[a][None])
            subln = attn_subln[a][None]
            w_qkv = attn_w_qkv[a].astype(BF16)
            w_o = attn_w_o[a].astype(BF16)
            qp, kp, vp = qkv_proj(xp, gm, w_qkv, 1024, 512)
            qs, ks, vs = qkv_proj(xs, gm, w_qkv, ts, 512)
            op = attn_prompt(qp, kp, vp, lams, subln, batch, seq, lam_init)
            os_ = attn_sample(qs.reshape(dec_b, dec_s, d), ks.reshape(dec_b, dec_s, d), vs.reshape(dec_b, dec_s, d),
                              cache_k.reshape(cache_k.shape[0], n_phys, PAGE_SIZE, d),
                              cache_v.reshape(cache_v.shape[0], n_phys, PAGE_SIZE, d),
                              page_table, lams, subln, a, lam_init)
            xp = matmul_residual(op, w_o, xp, 1024, 512)
            xs = matmul_residual(os_.reshape(ts, d), w_o, xs, ts, 512)
            k_p.append(kp.reshape(batch, seq, N_HEADS, 2, HEAD_DIM))
            v_p.append(vp.reshape(batch, seq, N_HEADS, HEAD_W))
            k_s.append(ks.reshape(dec_b, dec_s, N_HEADS, 2, HEAD_DIM))
            v_s.append(vs.reshape(dec_b, dec_s, N_HEADS, HEAD_W))
        else:
            s = i // 2
            ops = ssm_prep(ssm_A_re[s], ssm_A_im[s], ssm_log_dt[s], ssm_B_re[s], ssm_B_im[s],
                           ssm_C_re[s], ssm_C_im[s], dec_s)
            hp = rms_norm(xp, gm, 1024)
            hs = rms_norm(xs, gm, ts)
            up = hp.reshape(batch, n_chunks, SSM_CHUNK, g, gc).transpose(3, 1, 0, 2, 4).reshape(g, n_chunks * batch, SSM_CHUNK * gc)
            us = hs.reshape(dec_b, dec_s, g, gc).transpose(2, 0, 1, 3).reshape(g, dec_b, dec_s * gc)
            us = jnp.pad(us, ((0, 0), (0, 0), (0, (SSM_CHUNK - dec_s) * gc)))
            u_g = jnp.concatenate([up, us], axis=1)
            h0_g = jnp.concatenate([state_ssm_re[s], state_ssm_im[s]], axis=-1).transpose(1, 0, 2)
            d_tile = jnp.tile(ssm_D[s], (1, SSM_CHUNK))[:, None]
            z, hfin, hsfin = ssm_mix(u_g, h0_g, ops, d_tile, n_chunks, batch, dec_s)
            n_pr = n_chunks * batch
            zp = z[:, :n_pr].reshape(g, n_chunks, batch, SSM_CHUNK, gc).transpose(2, 1, 3, 0, 4).reshape(tp, d)
            zs = z[:, n_pr:, :dec_s * gc].reshape(g, dec_b, dec_s, gc).transpose(1, 2, 0, 3).reshape(ts, d)
            glu_w = ssm_glu_w[s].astype(BF16)
            glu_b = ssm_glu_b[s][None]
            xp = glu_residual(zp, glu_w, glu_b, xp, 1024, 512)
            xs = glu_residual(zs, glu_w, glu_b, xs, ts, 512)
            hfin = hfin.transpose(1, 0, 2)
            hsfin = hsfin.transpose(1, 0, 2)
            hr_p.append(hfin[..., :pst])
            hi_p.append(hfin[..., pst:])
            hr_s.append(hsfin[..., :pst])
            hi_s.append(hsfin[..., pst:])
        xp, xs = hier_moe([xp, xs], norm_ffn[i][None], moe_w_group[i], moe_b_group[i], moe_w_expert[i],
                          moe_b_expert[i], moe_w13[i].astype(BF16), moe_w2[i].astype(BF16))
    y_prompt = rms_norm(xp, norm_final[None], 1024).reshape(batch, seq, d)
    y_sample = rms_norm(xs, norm_final[None], ts).reshape(dec_b, dec_s, d)
    return (y_prompt, y_sample, jnp.stack(k_p), jnp.stack(v_p), jnp.stack(k_s), jnp.stack(v_s),
            jnp.stack(hr_p), jnp.stack(hi_p), jnp.stack(hr_s), jnp.stack(hi_s))
```

```python
import functools
import math

import jax
import jax.numpy as jnp
from jax import lax
from jax.experimental import pallas as pl
from jax.experimental.pallas import tpu as pltpu

F32 = jnp.float32
BF16 = jnp.bfloat16

RMS_EPS = 1e-6
SUBLN_EPS = 1e-5
HEAD_DIM = 128
HEAD_W = 2 * HEAD_DIM
N_HEADS = 8
PAGE_SIZE = 128
N_EXPERT_GROUPS = 4
EXPERTS_PER_GROUP = 8
N_EXPERTS = N_EXPERT_GROUPS * EXPERTS_PER_GROUP
EXPERT_FF = 1024
SSM_GROUP_CH = 16
SSM_STATE = 64
SSM_CHUNK = 16
LANES = 128
VMEM_LIMIT = 56 * 1024 * 1024


def _params(sem, vmem=VMEM_LIMIT):
    return pltpu.CompilerParams(dimension_semantics=sem, vmem_limit_bytes=vmem)


def _rms(x, g, eps):
    return x * lax.rsqrt(jnp.mean(x * x, axis=-1, keepdims=True) + eps) * g


def _split3(x):
    hi = x.astype(BF16)
    lo = (x - hi.astype(F32)).astype(BF16)
    return hi, lo


def _dot3(a, b, dims):
    ah, al = _split3(a)
    bh, bl = _split3(b)
    dn = (dims, ((), ()))
    d = functools.partial(lax.dot_general, dimension_numbers=dn, preferred_element_type=F32)
    return d(ah, bh) + d(al, bh) + d(ah, bl)


_NT = ((1,), (1,))
_NN = ((1,), (0,))


def _qkv_kernel(x_ref, g_ref, w_ref, q_ref, k_ref, v_ref, xn_ref, *, nj):
    j = pl.program_id(1)

    @pl.when(j == 0)
    def _():
        xn_ref[...] = _rms(x_ref[...], g_ref[...], RMS_EPS).astype(BF16)

    acc = jnp.dot(xn_ref[...], w_ref[...], preferred_element_type=F32)
    for t, o_ref in enumerate((q_ref, k_ref, v_ref)):
        @pl.when((j >= t * nj) & (j < (t + 1) * nj))
        def _(o_ref=o_ref):
            o_ref[...] = acc


def qkv_proj(x, g, w_bf, tm, tn):
    t, d = x.shape
    nj = d // tn

    def omap(part):
        return lambda i, j: (i, jnp.clip(j - part * nj, 0, nj - 1))

    return pl.pallas_call(
        functools.partial(_qkv_kernel, nj=nj),
        grid=(t // tm, 3 * nj),
        in_specs=[pl.BlockSpec((tm, d), lambda i, j: (i, 0)),
                  pl.BlockSpec((1, d), lambda i, j: (0, 0)),
                  pl.BlockSpec((d, tn), lambda i, j: (0, j))],
        out_specs=[pl.BlockSpec((tm, tn), omap(p)) for p in range(3)],
        out_shape=[jax.ShapeDtypeStruct((t, d), F32)] * 3,
        scratch_shapes=[pltpu.VMEM((tm, d), BF16)],
        compiler_params=_params(("arbitrary", "arbitrary")),
        name="qkv_proj",
    )(x, g, w_bf)


def _mm_res_kernel(a_ref, w_ref, x_ref, o_ref):
    o_ref[...] = x_ref[...] + jnp.dot(a_ref[...].astype(BF16), w_ref[...], preferred_element_type=F32)


def matmul_residual(a, w_bf, x, tm, tn):
    t, kd = a.shape
    n = w_bf.shape[1]
    return pl.pallas_call(
        _mm_res_kernel,
        grid=(t // tm, n // tn),
        in_specs=[pl.BlockSpec((tm, kd), lambda i, j: (i, 0)),
                  pl.BlockSpec((kd, tn), lambda i, j: (0, j)),
                  pl.BlockSpec((tm, tn), lambda i, j: (i, j))],
        out_specs=pl.BlockSpec((tm, tn), lambda i, j: (i, j)),
        out_shape=jax.ShapeDtypeStruct((t, n), F32),
        compiler_params=_params(("arbitrary", "arbitrary")),
        name="matmul_residual",
    )(a, w_bf, x)


def _glu_kernel(z_ref, w1_ref, w2_ref, b1_ref, b2_ref, x_ref, o_ref):
    z = z_ref[...]
    a = jnp.dot(z, w1_ref[...], preferred_element_type=F32) + b1_ref[...]
    b = jnp.dot(z, w2_ref[...], preferred_element_type=F32) + b2_ref[...]
    o_ref[...] = x_ref[...] + a * (1.0 / (1.0 + jnp.exp(-b)))


def glu_residual(z, w_bf, bias, x, tm, tn):
    t, d = z.shape
    nj = d // tn
    return pl.pallas_call(
        _glu_kernel,
        grid=(t // tm, nj),
        in_specs=[pl.BlockSpec((tm, d), lambda i, j: (i, 0)),
                  pl.BlockSpec((d, tn), lambda i, j: (0, j)),
                  pl.BlockSpec((d, tn), lambda i, j: (0, j + nj)),
                  pl.BlockSpec((1, tn), lambda i, j: (0, j)),
                  pl.BlockSpec((1, tn), lambda i, j: (0, j + nj)),
                  pl.BlockSpec((tm, tn), lambda i, j: (i, j))],
        out_specs=pl.BlockSpec((tm, tn), lambda i, j: (i, j)),
        out_shape=jax.ShapeDtypeStruct((t, d), F32),
        compiler_params=_params(("arbitrary", "arbitrary")),
        name="glu_residual",
    )(z, w_bf, w_bf, bias, bias, x)


def _norm_kernel(x_ref, g_ref, o_ref):
    o_ref[...] = _rms(x_ref[...], g_ref[...], RMS_EPS).astype(o_ref.dtype)


def rms_norm(x, g, tm):
    t, d = x.shape
    return pl.pallas_call(
        _norm_kernel,
        grid=(t // tm,),
        in_specs=[pl.BlockSpec((tm, d), lambda i: (i, 0)),
                  pl.BlockSpec((1, d), lambda i: (0, 0))],
        out_specs=pl.BlockSpec((tm, d), lambda i: (i, 0)),
        out_shape=jax.ShapeDtypeStruct((t, d), F32),
        compiler_params=_params(("arbitrary",)),
        name="rms_norm",
    )(x, g)


def _diff_lambda(lq1, lk1, lq2, lk2, lam_init):
    return (jnp.exp(jnp.sum(lq1[...] * lk1[...], axis=-1, keepdims=True))
            - jnp.exp(jnp.sum(lq2[...] * lk2[...], axis=-1, keepdims=True)) + lam_init)


def _attn_prompt_kernel(q_ref, k_ref, v_ref, lq1, lk1, lq2, lk2, subln_ref, o_ref,
                        kbf, vtb, acc_sc, *, tq, lam_init):
    seq = q_ref.shape[0]
    scale = HEAD_DIM ** -0.5
    kbf[...] = k_ref[...].astype(BF16)
    for j in range(seq // tq):
        vtb[j] = v_ref[j * tq:(j + 1) * tq, :].T.astype(BF16)
    lam = _diff_lambda(lq1, lk1, lq2, lk2, lam_init)
    krow = lax.broadcasted_iota(jnp.int32, (tq, tq), 0)
    qcol = lax.broadcasted_iota(jnp.int32, (tq, tq), 1)
    causal = krow <= qcol

    def q_body(qi, carry):
        q0 = pl.multiple_of(qi * tq, tq)
        qt = q_ref[pl.ds(q0, tq), :].astype(BF16)
        acc_sc[...] = jnp.zeros(acc_sc.shape, F32)

        def kv_step(kj, masked, st):
            k0 = pl.multiple_of(kj * tq, tq)
            kt = kbf[pl.ds(k0, tq), :]
            vt = vtb[kj]
            out = []
            for s in range(2):
                ls = slice(s * HEAD_DIM, (s + 1) * HEAD_DIM)
                sc = lax.dot_general(kt[:, ls], qt[:, ls], (_NT, ((), ())), preferred_element_type=F32) * scale
                if masked:
                    sc = jnp.where(causal, sc, -jnp.inf)
                m_prev, l_prev = st[2 * s], st[2 * s + 1]
                m_new = jnp.maximum(m_prev, jnp.max(sc, axis=0, keepdims=True))
                alpha = jnp.exp(m_prev - m_new)
                p = jnp.exp(sc - m_new)
                out += [m_new, alpha * l_prev + jnp.sum(p, axis=0, keepdims=True)]
                acc_sc[s] = alpha * acc_sc[s] + jnp.dot(vt, p.astype(BF16), preferred_element_type=F32)
            return tuple(out)

        ninf = jnp.full((1, tq), -jnp.inf, F32)
        zero = jnp.zeros((1, tq), F32)
        st = lax.fori_loop(0, qi, lambda kj, st: kv_step(kj, False, st), (ninf, zero, ninf, zero))
        _, l0, _, l1 = kv_step(qi, True, st)
        ot = acc_sc[0] / l0 - lam * (acc_sc[1] / l1)
        ot = ot * lax.rsqrt(jnp.mean(ot * ot, axis=0, keepdims=True) + SUBLN_EPS)
        o_ref[pl.ds(q0, tq), :] = (ot.T * subln_ref[...] * (1.0 - lam_init)).astype(o_ref.dtype)
        return carry

    lax.fori_loop(0, seq // tq, q_body, 0)


def attn_prompt(q, k, v, lams, subln, batch, seq, lam_init, tq=256):
    t, d = q.shape
    blk = pl.BlockSpec((seq, HEAD_W), lambda b, h: (b, h))
    vec = pl.BlockSpec((1, HEAD_DIM), lambda b, h: (0, 0))
    return pl.pallas_call(
        functools.partial(_attn_prompt_kernel, tq=tq, lam_init=lam_init),
        grid=(batch, N_HEADS),
        in_specs=[blk, blk, blk, vec, vec, vec, vec,
                  pl.BlockSpec((1, HEAD_W), lambda b, h: (0, 0))],
        out_specs=blk,
        out_shape=jax.ShapeDtypeStruct((t, d), BF16),
        scratch_shapes=[pltpu.VMEM((seq, HEAD_W), BF16), pltpu.VMEM((seq // tq, HEAD_W, tq), BF16),
                        pltpu.VMEM((2, HEAD_W, tq), F32)],
        compiler_params=_params(("arbitrary", "arbitrary")),
        name="attn_prompt",
    )(q, k, v, *lams, subln)


def _attn_sample_kernel(pt_ref, q_ref, kn_ref, vn_ref, *rest, n_new, n_pp, lam_init):
    del pt_ref
    kp_refs, vp_refs = rest[:n_pp], rest[n_pp:2 * n_pp]
    bias_ref, biasn_ref, lq1, lk1, lq2, lk2, subln_ref, o_ref, pbuf, m_sc, l_sc, acc_sc = rest[2 * n_pp:]
    p = pl.program_id(1)
    scale = HEAD_DIM ** -0.5
    qrows = q_ref[...].astype(BF16)

    @pl.when(p == 0)
    def _():
        m_sc[...] = jnp.full(m_sc.shape, -jnp.inf, F32)
        l_sc[...] = jnp.zeros(l_sc.shape, F32)
        acc_sc[...] = jnp.zeros(acc_sc.shape, F32)

    def attend(k_rows, v_rows, bias):
        r = k_rows.shape[0]
        st = lax.dot_general(k_rows.astype(BF16), qrows, (_NT, ((), ())), preferred_element_type=F32) * scale
        st = st.reshape(r // bias.shape[0], bias.shape[0], LANES) + bias[None]
        m_prev = m_sc[...]
        m_new = jnp.maximum(m_prev, jnp.max(jnp.max(st, axis=0), axis=0, keepdims=True))
        alpha = jnp.exp(m_prev - m_new)
        pr = jnp.exp(st - m_new)
        l_sc[...] = alpha * l_sc[...] + jnp.sum(jnp.sum(pr, axis=0), axis=0, keepdims=True)
        m_sc[...] = m_new
        pbuf[0:r, :] = pr.reshape(r, LANES)
        pv = pbuf[pl.ds(0, r // 2, stride=2), :] + pbuf[pl.ds(1, r // 2, stride=2), :]
        vt = v_rows.T.astype(BF16)
        acc_sc[...] = alpha * acc_sc[...] + jnp.dot(vt, pv.astype(BF16), preferred_element_type=F32)

    for kp_ref, vp_ref in zip(kp_refs, vp_refs):
        attend(kp_ref[...], vp_ref[...], bias_ref[...])

    @pl.when(p == pl.num_programs(1) - 1)
    def _():
        attend(kn_ref[...], vn_ref[...], biasn_ref[...])
        lam = _diff_lambda(lq1, lk1, lq2, lk2, lam_init)
        on = acc_sc[...] / l_sc[...]
        d = on - lam * pltpu.roll(on, LANES - n_new, 1)
        d = d * lax.rsqrt(jnp.mean(d * d, axis=0, keepdims=True) + SUBLN_EPS)
        o = d.T * subln_ref[...] * (1.0 - lam_init)
        for h in range(N_HEADS):
            o_ref[:, h * HEAD_W:(h + 1) * HEAD_W] = o[h * 2 * n_new:h * 2 * n_new + n_new]


def attn_sample(q, k_new, v_new, cache_k, cache_v, page_table, lams, subln, layer, lam_init, n_pp=4):
    b, n_new, d = q.shape
    n_pages = page_table.shape[1]
    n_phys = cache_k.shape[1]
    kh = 2 * N_HEADS
    nrow = N_HEADS * 2 * n_new
    n_slot = 16
    assert nrow <= LANES and n_new <= n_slot
    k_view = cache_k.reshape(cache_k.shape[0] * n_phys, PAGE_SIZE * kh, HEAD_DIM)
    v_view = cache_v.reshape(cache_v.shape[0] * n_phys, PAGE_SIZE * N_HEADS, HEAD_W)
    qrows = q.reshape(b, n_new, kh, HEAD_DIM).transpose(0, 2, 1, 3).reshape(b, nrow, HEAD_DIM)
    qrows = jnp.pad(qrows, ((0, 0), (0, LANES - nrow), (0, 0)))
    kn = jnp.pad(k_new, ((0, 0), (0, n_slot - n_new), (0, 0))).reshape(b, n_slot * kh, HEAD_DIM)
    vn = jnp.pad(v_new, ((0, 0), (0, n_slot - n_new), (0, 0))).reshape(b, n_slot * N_HEADS, HEAD_W)
    lane = jnp.arange(LANES)
    own = (jnp.arange(kh)[:, None] == (lane // n_new)[None, :]) | (lane >= nrow)[None, :]
    bias = jnp.where(own, 0.0, -jnp.inf).astype(F32)
    tokn = jnp.arange(n_slot)[:, None, None]
    seen = (tokn < n_new) & (tokn <= (lane % n_new)[None, None, :])
    bias_new = jnp.where((own[None] & seen) | (lane >= nrow)[None, None, :], 0.0, -jnp.inf)
    bias_new = bias_new.astype(F32).reshape(n_slot * kh, LANES)
    tok = pl.BlockSpec((None, n_new, d), lambda i, p, pt: (i, 0, 0))

    def per_seq(rows, width):
        return pl.BlockSpec((None, rows, width), lambda i, p, pt: (i, 0, 0))

    def page_spec(rows, width, k):
        return pl.BlockSpec((None, rows, width),
                            lambda i, p, pt: (layer * n_phys + pt[i * n_pages + p * n_pp + k], 0, 0))

    def const(rows, width):
        return pl.BlockSpec((rows, width), lambda i, p, pt: (0, 0))

    grid_spec = pltpu.PrefetchScalarGridSpec(
        num_scalar_prefetch=1,
        grid=(b, n_pages // n_pp),
        in_specs=[per_seq(LANES, HEAD_DIM), per_seq(n_slot * kh, HEAD_DIM), per_seq(n_slot * N_HEADS, HEAD_W)]
                 + [page_spec(PAGE_SIZE * kh, HEAD_DIM, k) for k in range(n_pp)]
                 + [page_spec(PAGE_SIZE * N_HEADS, HEAD_W, k) for k in range(n_pp)]
                 + [const(kh, LANES), const(n_slot * kh, LANES)]
                 + [const(1, HEAD_DIM)] * 4 + [const(1, HEAD_W)],
        out_specs=tok,
        scratch_shapes=[pltpu.VMEM((PAGE_SIZE * kh, LANES), F32), pltpu.VMEM((1, LANES), F32),
                        pltpu.VMEM((1, LANES), F32), pltpu.VMEM((HEAD_W, LANES), F32)])
    return pl.pallas_call(
        functools.partial(_attn_sample_kernel, n_new=n_new, n_pp=n_pp, lam_init=lam_init),
        grid_spec=grid_spec,
        out_shape=jax.ShapeDtypeStruct((b, n_new, d), F32),
        compiler_params=_params(("arbitrary", "arbitrary")),
        name="attn_sample",
    )(page_table.reshape(-1), qrows, kn, vn, *([k_view] * n_pp), *([v_view] * n_pp), bias, bias_new, *lams, subln)


def _router_kernel(x_ref, g_ref, wh_ref, wl_ref, b_ref, xn_ref, ids_ref, gates_ref):
    xn = _rms(x_ref[...], g_ref[...], RMS_EPS)
    xn_ref[...] = xn
    xh, xl = _split3(xn)
    d = functools.partial(jnp.dot, preferred_element_type=F32)
    logits = d(xh, wh_ref[...]) + d(xl, wh_ref[...]) + d(xh, wl_ref[...]) + b_ref[...]
    lane = lax.broadcasted_iota(jnp.int32, logits.shape, 1)
    ninf = -jnp.inf
    gl = jnp.where(lane < N_EXPERT_GROUPS, logits, ninf)
    gmax = jnp.max(gl, axis=-1, keepdims=True)
    g_val = 1.0 / jnp.sum(jnp.exp(gl - gmax), axis=-1, keepdims=True)
    g_idx = jnp.min(jnp.where(gl == gmax, lane, LANES), axis=-1, keepdims=True)
    lo = N_EXPERT_GROUPS + EXPERTS_PER_GROUP * g_idx
    el = jnp.where((lane >= lo) & (lane < lo + EXPERTS_PER_GROUP), logits, ninf)
    e1 = jnp.max(el, axis=-1, keepdims=True)
    i1 = jnp.min(jnp.where(el == e1, lane, LANES), axis=-1, keepdims=True)
    el2 = jnp.where(lane == i1, ninf, el)
    e2 = jnp.max(el2, axis=-1, keepdims=True)
    i2 = jnp.min(jnp.where(el2 == e2, lane, LANES), axis=-1, keepdims=True)
    r = jnp.exp(e2 - e1)
    w1 = g_val / (1.0 + r)
    w2 = g_val * r / (1.0 + r)
    ids_ref[...] = jnp.where(lane == 0, i1 - N_EXPERT_GROUPS, jnp.where(lane == 1, i2 - N_EXPERT_GROUPS, 0))
    gates_ref[...] = jnp.where(lane == 0, w1, jnp.where(lane == 1, w2, 0.0))


def moe_route(x, g, w_hi, w_lo, bias, tm):
    t, d = x.shape
    row = lambda i: (i, 0)
    const = lambda i: (0, 0)
    return pl.pallas_call(
        _router_kernel,
        grid=(t // tm,),
        in_specs=[pl.BlockSpec((tm, d), row), pl.BlockSpec((1, d), const),
                  pl.BlockSpec((d, LANES), const), pl.BlockSpec((d, LANES), const),
                  pl.BlockSpec((1, LANES), const)],
        out_specs=[pl.BlockSpec((tm, d), row), pl.BlockSpec((tm, LANES), row), pl.BlockSpec((tm, LANES), row)],
        out_shape=[jax.ShapeDtypeStruct((t, d), F32), jax.ShapeDtypeStruct((t, LANES), jnp.int32),
                   jax.ShapeDtypeStruct((t, LANES), F32)],
        compiler_params=_params(("arbitrary",)),
        name="moe_route",
    )(x, g, w_hi, w_lo, bias)


def _row_gather_start(idx_ref, base, n, src_hbm, dst, sem):
    def body(r, c):
        pltpu.make_async_copy(src_hbm.at[pl.ds(idx_ref[base + r], 1), :], dst.at[pl.ds(r, 1), :], sem).start()
        return c
    lax.fori_loop(0, n, body, 0, unroll=8)


def _row_gather_wait(dst, sem):
    pltpu.make_async_copy(dst, dst, sem).wait()


def _expert_kernel(bexp_ref, rtok_ref, nused_ref, xn_hbm, w13_ref, w2_ref, ys_ref, xbuf, sem, *, tm):
    del bexp_ref
    rb = pl.program_id(0)
    nused = nused_ref[0]

    @pl.when(rb == 0)
    def _():
        _row_gather_start(rtok_ref, 0, tm, xn_hbm, xbuf.at[0], sem.at[0])

    @pl.when(rb + 1 < nused)
    def _():
        nxt = (rb + 1) % 2
        _row_gather_start(rtok_ref, (rb + 1) * tm, tm, xn_hbm, xbuf.at[nxt], sem.at[nxt])

    @pl.when(rb < nused)
    def _():
        slot = rb % 2
        _row_gather_wait(xbuf.at[slot], sem.at[slot])
        x = xbuf[slot].astype(BF16)
        h = jnp.dot(x, w13_ref[...], preferred_element_type=F32)
        ga, up = h[:, :EXPERT_FF], h[:, EXPERT_FF:]
        a = ga * (1.0 / (1.0 + jnp.exp(-ga))) * up
        ys_ref[...] = jnp.dot(a.astype(BF16), w2_ref[...], preferred_element_type=F32)

    @pl.when(rb >= nused)
    def _():
        ys_ref[...] = jnp.zeros(ys_ref.shape, F32)


def moe_experts(xn, w13_bf, w2_bf, block_exp, row_tok, n_used, tm):
    t, d = xn.shape
    n_blocks = block_exp.shape[0]
    grid_spec = pltpu.PrefetchScalarGridSpec(
        num_scalar_prefetch=3,
        grid=(n_blocks,),
        in_specs=[pl.BlockSpec(memory_space=pl.ANY),
                  pl.BlockSpec((None, d, 2 * EXPERT_FF), lambda rb, be, rt, nu: (be[rb], 0, 0)),
                  pl.BlockSpec((None, EXPERT_FF, d), lambda rb, be, rt, nu: (be[rb], 0, 0))],
        out_specs=pl.BlockSpec((tm, d), lambda rb, be, rt, nu: (rb, 0)),
        scratch_shapes=[pltpu.VMEM((2, tm, d), F32), pltpu.SemaphoreType.DMA((2,))])
    return pl.pallas_call(
        functools.partial(_expert_kernel, tm=tm),
        grid_spec=grid_spec,
        out_shape=jax.ShapeDtypeStruct((n_blocks * tm, d), F32),
        compiler_params=_params(("arbitrary",)),
        name="moe_experts",
    )(block_exp, row_tok, n_used, xn, w13_bf, w2_bf)


def _combine_kernel(pos_ref, ys_hbm, x_ref, gates_ref, o_ref, ybuf, sem, *, tm):
    i = pl.program_id(0)
    n = pl.num_programs(0)

    def start(tile, slot):
        _row_gather_start(pos_ref, tile * 2 * tm, 2 * tm, ys_hbm, ybuf.at[slot], sem.at[slot])

    @pl.when(i == 0)
    def _():
        start(0, 0)

    @pl.when(i + 1 < n)
    def _():
        start(i + 1, (i + 1) % 2)

    slot = i % 2
    _row_gather_wait(ybuf.at[slot], sem.at[slot])
    g = gates_ref[...]
    o_ref[...] = x_ref[...] + (g[:, 0:1] * ybuf[slot, :tm] + g[:, 1:2] * ybuf[slot, tm:])


def moe_combine(ys, x, gates, pos, tm):
    t, d = x.shape
    pos_tiles = pos.reshape(t // tm, tm, 2).transpose(0, 2, 1).reshape(-1)
    grid_spec = pltpu.PrefetchScalarGridSpec(
        num_scalar_prefetch=1,
        grid=(t // tm,),
        in_specs=[pl.BlockSpec(memory_space=pl.ANY),
                  pl.BlockSpec((tm, d), lambda i, ps: (i, 0)),
                  pl.BlockSpec((tm, LANES), lambda i, ps: (i, 0))],
        out_specs=pl.BlockSpec((tm, d), lambda i, ps: (i, 0)),
        scratch_shapes=[pltpu.VMEM((2, 2 * tm, d), F32), pltpu.SemaphoreType.DMA((2,))])
    return pl.pallas_call(
        functools.partial(_combine_kernel, tm=tm),
        grid_spec=grid_spec,
        out_shape=jax.ShapeDtypeStruct((t, d), F32),
        compiler_params=_params(("arbitrary",)),
        name="moe_combine",
    )(pos_tiles, ys, x, gates)


def hier_moe(xs, g, w_group, b_group, w_expert, b_expert, w13_bf, w2_bf, tm_blk=256):
    d = xs[0].shape[1]
    pad = LANES - N_EXPERT_GROUPS - N_EXPERTS
    w_cat = jnp.concatenate([w_group, w_expert, jnp.zeros((d, pad), F32)], axis=1)
    b_cat = jnp.concatenate([b_group, b_expert, jnp.zeros((pad,), F32)])[None]
    w_hi = w_cat.astype(BF16)
    w_lo = (w_cat - w_hi.astype(F32)).astype(BF16)
    routed = [moe_route(x, g, w_hi, w_lo, b_cat, min(512, x.shape[0])) for x in xs]
    xn = jnp.concatenate([r[0] for r in routed], axis=0)
    ids = jnp.concatenate([r[1][:, :2] for r in routed], axis=0)
    t = xn.shape[0]
    n_assign = 2 * t
    flat_e = ids.reshape(n_assign)
    onehot = (flat_e[:, None] == jnp.arange(N_EXPERTS, dtype=jnp.int32)[None, :]).astype(jnp.int32)
    csum = jnp.cumsum(onehot, axis=0)
    counts = csum[-1]
    rank = jnp.take_along_axis(csum, flat_e[:, None], axis=1)[:, 0] - 1
    padded = ((counts + tm_blk - 1) // tm_blk) * tm_blk
    pad_end = jnp.cumsum(padded)
    pad_start = pad_end - padded
    dest = pad_start[flat_e] + rank
    n_blocks = -(-n_assign // tm_blk) + N_EXPERTS
    flat_t = jnp.arange(n_assign, dtype=jnp.int32) // 2
    row_tok = jnp.zeros((n_blocks * tm_blk,), jnp.int32).at[dest].set(flat_t)
    block_exp = jnp.minimum(jnp.searchsorted(pad_end, jnp.arange(n_blocks, dtype=jnp.int32) * tm_blk, side='right'),
                            N_EXPERTS - 1).astype(jnp.int32)
    n_used = (pad_end[-1:] // tm_blk).astype(jnp.int32)
    ys = moe_experts(xn, w13_bf, w2_bf, block_exp, row_tok, n_used, tm_blk)
    pos = dest.reshape(t, 2)
    outs, off = [], 0
    for x, r in zip(xs, routed):
        n = x.shape[0]
        outs.append(moe_combine(ys, x, r[2], pos[off:off + n], min(128, n)))
        off += n
    return outs


def _cmul(ar, ai, br, bi):
    return ar * br - ai * bi, ar * bi + ai * br


def _ssm_prep_kernel(are_ref, aim_ref, ldt_ref, bre_ref, bim_ref, cre_ref, cim_ref,
                     m_ref, bc_ref, cc_ref, a16_ref, a4_ref, *, n_new):
    p = SSM_STATE
    nc = SSM_CHUNK
    a_re, a_im = are_ref[...], aim_ref[...]
    dt = jnp.exp(ldt_ref[...])
    mag = jnp.exp(dt * a_re)
    ab_re = mag * jnp.cos(dt * a_im)
    ab_im = mag * jnp.sin(dt * a_im)
    den = a_re * a_re + a_im * a_im
    nr = ab_re - 1.0
    fr = (nr * a_re + ab_im * a_im) / den
    fi = (ab_im * a_re - nr * a_im) / den
    bt_re, bt_im = bre_ref[...], bim_ref[...]
    bb_re = fr * bt_re - fi * bt_im
    bb_im = fr * bt_im + fi * bt_re
    c_re, c_im = cre_ref[...], cim_ref[...]
    pw = [(jnp.ones_like(ab_re), jnp.zeros_like(ab_re))]
    for _ in range(nc):
        pw.append(_cmul(pw[-1][0], pw[-1][1], ab_re, ab_im))
    wst = []
    for tau in range(nc + 1):
        wr, wi = _cmul(c_re, c_im, pw[tau][0], pw[tau][1])
        wst.append(jnp.concatenate([wr, -wi], axis=1))
    wstack = jnp.concatenate(wst[:nc], axis=0)
    bs_t = jnp.concatenate([bb_re, bb_im], axis=1)
    q = _dot3(bs_t, wstack, _NT)
    qpad = jnp.concatenate([jnp.zeros_like(q), q], axis=1)
    w = nc * SSM_GROUP_CH
    rows = []
    for s in range(nc):
        off = (nc - s) * SSM_GROUP_CH
        rows.append(qpad[:, off:off + w])
    m_ref[...] = jnp.concatenate(rows, axis=0).astype(m_ref.dtype)
    bc = []
    for s in range(nc):
        xr, xi = _cmul(bb_re, bb_im, pw[nc - 1 - s][0], pw[nc - 1 - s][1])
        bc.append(jnp.concatenate([xr, xi], axis=1))
    bc_ref[...] = jnp.concatenate(bc, axis=0).astype(bc_ref.dtype)
    cc_ref[...] = jnp.concatenate(wst[1:], axis=0).astype(cc_ref.dtype)
    a16_ref[...] = jnp.concatenate([pw[nc][0], pw[nc][1]], axis=1)
    a4_ref[...] = jnp.concatenate([pw[n_new][0], pw[n_new][1]], axis=1)


def ssm_prep(a_re, a_im, log_dt, b_re, b_im, c_re, c_im, n_new):
    g, p = a_re.shape
    gc = SSM_GROUP_CH
    w = SSM_CHUNK * gc
    vec = pl.BlockSpec((None, 1, p), lambda i: (i, 0, 0))
    mat = pl.BlockSpec((None, gc, p), lambda i: (i, 0, 0))
    bt_re = b_re.transpose(0, 2, 1)
    bt_im = b_im.transpose(0, 2, 1)
    return pl.pallas_call(
        functools.partial(_ssm_prep_kernel, n_new=n_new),
        grid=(g,),
        in_specs=[vec, vec, pl.BlockSpec((None, 1, 1), lambda i: (i, 0, 0)), mat, mat, mat, mat],
        out_specs=[pl.BlockSpec((None, w, w), lambda i: (i, 0, 0)),
                   pl.BlockSpec((None, w, 2 * p), lambda i: (i, 0, 0)),
                   pl.BlockSpec((None, w, 2 * p), lambda i: (i, 0, 0)),
                   pl.BlockSpec((None, 1, 2 * p), lambda i: (i, 0, 0)),
                   pl.BlockSpec((None, 1, 2 * p), lambda i: (i, 0, 0))],
        out_shape=[jax.ShapeDtypeStruct((g, w, w), BF16),
                   jax.ShapeDtypeStruct((g, w, 2 * p), BF16),
                   jax.ShapeDtypeStruct((g, w, 2 * p), BF16),
                   jax.ShapeDtypeStruct((g, 1, 2 * p), F32),
                   jax.ShapeDtypeStruct((g, 1, 2 * p), F32)],
        compiler_params=_params(("arbitrary",)),
        name="ssm_prep",
    )(a_re[:, None], a_im[:, None], log_dt[:, None, None], bt_re, bt_im, c_re, c_im)


def _ssm_x_kernel(u_ref, bc_ref, x_ref, *, n_prompt, n_new):
    u = u_ref[...].astype(BF16)
    bc = bc_ref[...]
    x_ref[:n_prompt] = jnp.dot(u[:n_prompt], bc, preferred_element_type=F32)
    k = n_new * SSM_GROUP_CH
    x_ref[n_prompt:] = jnp.dot(u[n_prompt:, :k], bc[bc.shape[0] - k:], preferred_element_type=F32)


def _ssm_scan_kernel(x_ref, h0_ref, a16_ref, a4_ref, hs_ref, hfin_ref, hsfin_ref, *, n_chunks, batch):
    p = SSM_STATE

    def coef(a):
        ar, ai = a[:, :, :p], a[:, :, p:]
        return jnp.concatenate([ar, ar], axis=-1), jnp.concatenate([-ai, ai], axis=-1)

    def step(h, c1, c2):
        return c1 * h + c2 * pltpu.roll(h, p, 2)

    c1, c2 = coef(a16_ref[...])
    n_prompt = n_chunks * batch

    def body(c, h):
        r0 = pl.multiple_of(c * batch, batch)
        hs_ref[:, pl.ds(r0, batch), :] = h
        return step(h, c1, c2) + x_ref[:, pl.ds(r0, batch), :]

    h0 = jnp.zeros((x_ref.shape[0], batch, 2 * p), F32)
    hfin_ref[...] = lax.fori_loop(0, n_chunks, body, h0)
    d1, d2 = coef(a4_ref[...])
    hs0 = h0_ref[...]
    hs_ref[:, n_prompt:, :] = hs0
    hsfin_ref[...] = step(hs0, d1, d2) + x_ref[:, n_prompt:, :]


def _ssm_y_kernel(u_ref, hs_ref, m_ref, cc_ref, d_ref, z_ref):
    u = u_ref[...]
    y = jnp.dot(u.astype(BF16), m_ref[...], preferred_element_type=F32)
    y = y + lax.dot_general(hs_ref[...].astype(BF16), cc_ref[...], (_NT, ((), ())), preferred_element_type=F32)
    y = y + d_ref[...] * u
    c = math.sqrt(2.0 / math.pi)
    z = 0.5 * y * (1.0 + jnp.tanh(c * (y + 0.044715 * (y * y * y))))
    z_ref[...] = z.astype(z_ref.dtype)


def ssm_mix(u_g, h0_g, ops, d_tile, n_chunks, batch, n_new, g_blk=16):
    m_g, bc_g, cc_g, a16, a4 = ops
    g, r, w = u_g.shape
    p2 = 2 * SSM_STATE
    n_prompt = n_chunks * batch
    n_s = r - n_prompt
    per_g = lambda shape: pl.BlockSpec((None,) + shape, lambda i: (i,) + (0,) * len(shape))
    x = pl.pallas_call(
        functools.partial(_ssm_x_kernel, n_prompt=n_prompt, n_new=n_new),
        grid=(g,),
        in_specs=[per_g((r, w)), per_g((w, p2))],
        out_specs=per_g((r, p2)),
        out_shape=jax.ShapeDtypeStruct((g, r, p2), F32),
        compiler_params=_params(("arbitrary",)),
        name="ssm_chunk_state",
    )(u_g, bc_g)
    blk = lambda rows: pl.BlockSpec((g_blk, rows, p2), lambda i: (i, 0, 0))
    hs, hfin, hsfin = pl.pallas_call(
        functools.partial(_ssm_scan_kernel, n_chunks=n_chunks, batch=batch),
        grid=(g // g_blk,),
        in_specs=[blk(r), blk(n_s), blk(1), blk(1)],
        out_specs=[blk(r), blk(batch), blk(n_s)],
        out_shape=[jax.ShapeDtypeStruct((g, r, p2), F32),
                   jax.ShapeDtypeStruct((g, batch, p2), F32),
                   jax.ShapeDtypeStruct((g, n_s, p2), F32)],
        compiler_params=_params(("arbitrary",)),
        name="ssm_scan",
    )(x, h0_g, a16, a4)
    z = pl.pallas_call(
        _ssm_y_kernel,
        grid=(g,),
        in_specs=[per_g((r, w)), per_g((r, p2)), per_g((w, w)), per_g((w, p2)), per_g((1, w))],
        out_specs=per_g((r, w)),
        out_shape=jax.ShapeDtypeStruct((g, r, w), BF16),
        compiler_params=_params(("arbitrary",)),
        name="ssm_output",
    )(u_g, hs, m_g, cc_g, d_tile)
    return z, hfin, hsfin


def kernel(x_prompt, x_sample, cache_k, cache_v, state_ssm_re, state_ssm_im, page_table, norm_mix, norm_ffn, norm_final, attn_w_qkv, attn_lambda_q1, attn_lambda_k1, attn_lambda_q2, attn_lambda_k2, attn_subln, attn_w_o, ssm_A_re, ssm_A_im, ssm_log_dt, ssm_B_re, ssm_B_im, ssm_C_re, ssm_C_im, ssm_D, ssm_glu_w, ssm_glu_b, moe_w_group, moe_b_group, moe_w_expert, moe_b_expert, moe_w13, moe_w2):
    batch, seq, d = x_prompt.shape
    dec_b, dec_s, _ = x_sample.shape
    depth = norm_mix.shape[0]
    tp, ts = batch * seq, dec_b * dec_s
    xp = x_prompt.reshape(tp, d)
    xs = x_sample.reshape(ts, d)
    g, pst, gc = ssm_B_re.shape[1:]
    n_chunks = seq // SSM_CHUNK
    k_p, v_p, k_s, v_s = [], [], [], []
    hr_p, hi_p, hr_s, hi_s = [], [], [], []
    for i in range(depth):
        gm = norm_mix[i][None]
        if i % 2 == 0:
            a = i // 2
            lam_init = 0.8 - 0.6 * math.exp(-0.3 * i)
            lams = (attn_lambda_q1[a][None], attn_lambda_k1[a][None], attn_lambda_q2[a][None], attn_lambda_k2[a][None])
            subln = attn_subln[a][None]
            w_qkv = attn_w_qkv[a].astype(BF16)
            w_o = attn_w_o[a].astype(BF16)
            qp, kp, vp = qkv_proj(xp, gm, w_qkv, 1024, 512)
            qs, ks, vs = qkv_proj(xs, gm, w_qkv, ts, 512)
            op = attn_prompt(qp, kp, vp, lams, subln, batch, seq, lam_init)
            os_ = attn_sample(qs.reshape(dec_b, dec_s, d), ks.reshape(dec_b, dec_s, d), vs.reshape(dec_b, dec_s, d),
                              cache_k, cache_v, page_table, lams, subln, a, lam_init)
            xp = matmul_residual(op, w_o, xp, 1024, 512)
            xs = matmul_residual(os_.reshape(ts, d), w_o, xs, ts, 512)
            k_p.append(kp.reshape(batch, seq, N_HEADS, 2, HEAD_DIM))
            v_p.append(vp.reshape(batch, seq, N_HEADS, HEAD_W))
            k_s.append(ks.reshape(dec_b, dec_s, N_HEADS, 2, HEAD_DIM))
            v_s.append(vs.reshape(dec_b, dec_s, N_HEADS, HEAD_W))
        else:
            s = i // 2
            ops = ssm_prep(ssm_A_re[s], ssm_A_im[s], ssm_log_dt[s], ssm_B_re[s], ssm_B_im[s],
                           ssm_C_re[s], ssm_C_im[s], dec_s)
            hp = rms_norm(xp, gm, 1024)
            hs = rms_norm(xs, gm, ts)
            up = hp.reshape(batch, n_chunks, SSM_CHUNK, g, gc).transpose(3, 1, 0, 2, 4).reshape(g, n_chunks * batch, SSM_CHUNK * gc)
            us = hs.reshape(dec_b, dec_s, g, gc).transpose(2, 0, 1, 3).reshape(g, dec_b, dec_s * gc)
            us = jnp.pad(us, ((0, 0), (0, 0), (0, (SSM_CHUNK - dec_s) * gc)))
            u_g = jnp.concatenate([up, us], axis=1)
            h0_g = jnp.concatenate([state_ssm_re[s], state_ssm_im[s]], axis=-1).transpose(1, 0, 2)
            d_tile = jnp.tile(ssm_D[s], (1, SSM_CHUNK))[:, None]
            z, hfin, hsfin = ssm_mix(u_g, h0_g, ops, d_tile, n_chunks, batch, dec_s)
            n_pr = n_chunks * batch
            zp = z[:, :n_pr].reshape(g, n_chunks, batch, SSM_CHUNK, gc).transpose(2, 1, 3, 0, 4).reshape(tp, d)
            zs = z[:, n_pr:, :dec_s * gc].reshape(g, dec_b, dec_s, gc).transpose(1, 2, 0, 3).reshape(ts, d)
            glu_w = ssm_glu_w[s].astype(BF16)
            glu_b = ssm_glu_b[s][None]
            xp = glu_residual(zp, glu_w, glu_b, xp, 1024, 512)
            xs = glu_residual(zs, glu_w, glu_b, xs, ts, 512)
            hfin = hfin.transpose(1, 0, 2)
            hsfin = hsfin.transpose(1, 0, 2)
            hr_p.append(hfin[..., :pst])
            hi_p.append(hfin[..., pst:])
            hr_s.append(hsfin[..., :pst])
            hi_s.append(hsfin[..., pst:])
        xp, xs = hier_moe([xp, xs], norm_ffn[i][None], moe_w_group[i], moe_b_group[i], moe_w_expert[i],
                          moe_b_expert[i], moe_w13[i].astype(BF16), moe_w2[i].astype(BF16))
    y_prompt = rms_norm(xp, norm_final[None], 1024).reshape(batch, seq, d)
    y_sample = rms_norm(xs, norm_final[None], ts).reshape(dec_b, dec_s, d)
    return (y_prompt, y_sample, jnp.stack(k_p), jnp.stack(v_p), jnp.stack(k_s), jnp.stack(v_s),
            jnp.stack(hr_p), jnp.stack(hi_p), jnp.stack(hr_s), jnp.stack(hi_s))
```

```python
import functools
import math

import numpy as np
import jax
import jax.numpy as jnp
from jax import lax
from jax.experimental import pallas as pl
from jax.experimental.pallas import tpu as pltpu

F32 = jnp.float32
BF16 = jnp.bfloat16

RMS_EPS = 1e-6
SUBLN_EPS = 1e-5
HEAD_DIM = 128
HEAD_W = 2 * HEAD_DIM
N_HEADS = 8
PAGE_SIZE = 128
N_EXPERT_GROUPS = 4
EXPERTS_PER_GROUP = 8
N_EXPERTS = N_EXPERT_GROUPS * EXPERTS_PER_GROUP
EXPERT_FF = 1024
SSM_GROUP_CH = 16
SSM_STATE = 64
SSM_CHUNK = 16
LANES = 128
VMEM_LIMIT = 56 * 1024 * 1024


def _params(sem, vmem=VMEM_LIMIT):
    return pltpu.CompilerParams(dimension_semantics=sem, vmem_limit_bytes=vmem)


def _rms(x, g, eps):
    return x * lax.rsqrt(jnp.mean(x * x, axis=-1, keepdims=True) + eps) * g


def _split3(x):
    hi = x.astype(BF16)
    lo = (x - hi.astype(F32)).astype(BF16)
    return hi, lo


def _dot3(a, b, dims):
    ah, al = _split3(a)
    bh, bl = _split3(b)
    dn = (dims, ((), ()))
    d = functools.partial(lax.dot_general, dimension_numbers=dn, preferred_element_type=F32)
    return d(ah, bh) + d(al, bh) + d(ah, bl)


_NT = ((1,), (1,))
_NN = ((1,), (0,))


def _qkv_kernel(x_ref, g_ref, w_ref, q_ref, kh_ref, kn_ref, v_ref, xn_ref, *, nj):
    j = pl.program_id(1)
    tn = w_ref.shape[1]

    @pl.when(j == 0)
    def _():
        xn_ref[...] = _rms(x_ref[...], g_ref[...], RMS_EPS).astype(BF16)

    acc = jnp.dot(xn_ref[...], w_ref[...], preferred_element_type=F32)

    @pl.when(j < nj)
    def _():
        q_ref[...] = acc.astype(q_ref.dtype)

    for jj in range(nj):
        @pl.when(j == nj + jj)
        def _(jj=jj):
            kh_ref[...] = acc.astype(kh_ref.dtype)
            for i in range(tn // HEAD_DIM):
                kn_ref[:, jj * (tn // HEAD_DIM) + i, :] = acc[:, i * HEAD_DIM:(i + 1) * HEAD_DIM]

    @pl.when(j >= 2 * nj)
    def _():
        v_ref[...] = acc


def qkv_proj(x, g, w_bf, tm, tn):
    t, d = x.shape
    nj = d // tn

    def omap(part):
        return lambda i, j: (i, jnp.clip(j - part * nj, 0, nj - 1))

    return pl.pallas_call(
        functools.partial(_qkv_kernel, nj=nj),
        grid=(t // tm, 3 * nj),
        in_specs=[pl.BlockSpec((tm, d), lambda i, j: (i, 0)),
                  pl.BlockSpec((1, d), lambda i, j: (0, 0)),
                  pl.BlockSpec((d, tn), lambda i, j: (0, j))],
        out_specs=[pl.BlockSpec((tm, tn), omap(0)), pl.BlockSpec((tm, tn), omap(1)),
                   pl.BlockSpec((tm, d // HEAD_DIM, HEAD_DIM), lambda i, j: (i, 0, 0)),
                   pl.BlockSpec((tm, tn), omap(2))],
        out_shape=[jax.ShapeDtypeStruct((t, d), BF16), jax.ShapeDtypeStruct((t, d), BF16),
                   jax.ShapeDtypeStruct((t, d // HEAD_DIM, HEAD_DIM), F32), jax.ShapeDtypeStruct((t, d), F32)],
        scratch_shapes=[pltpu.VMEM((tm, d), BF16)],
        compiler_params=_params(("arbitrary", "arbitrary")),
        name="qkv_proj",
    )(x, g, w_bf)


def _mm_res_kernel(a_ref, w_ref, x_ref, o_ref):
    o_ref[...] = x_ref[...] + jnp.dot(a_ref[...].astype(BF16), w_ref[...], preferred_element_type=F32)


def matmul_residual(a, w_bf, x, tm, tn):
    t, kd = a.shape
    n = w_bf.shape[1]
    return pl.pallas_call(
        _mm_res_kernel,
        grid=(t // tm, n // tn),
        in_specs=[pl.BlockSpec((tm, kd), lambda i, j: (i, 0)),
                  pl.BlockSpec((kd, tn), lambda i, j: (0, j)),
                  pl.BlockSpec((tm, tn), lambda i, j: (i, j))],
        out_specs=pl.BlockSpec((tm, tn), lambda i, j: (i, j)),
        out_shape=jax.ShapeDtypeStruct((t, n), F32),
        compiler_params=_params(("arbitrary", "arbitrary")),
        name="matmul_residual",
    )(a, w_bf, x)


def _glu_kernel(z_ref, w1_ref, w2_ref, b1_ref, b2_ref, x_ref, o_ref):
    z = z_ref[...]
    a = jnp.dot(z, w1_ref[...], preferred_element_type=F32) + b1_ref[...]
    b = jnp.dot(z, w2_ref[...], preferred_element_type=F32) + b2_ref[...]
    o_ref[...] = x_ref[...] + a * (1.0 / (1.0 + jnp.exp(-b)))


def glu_residual(z, w_bf, bias, x, tm, tn):
    t, d = z.shape
    nj = d // tn
    return pl.pallas_call(
        _glu_kernel,
        grid=(t // tm, nj),
        in_specs=[pl.BlockSpec((tm, d), lambda i, j: (i, 0)),
                  pl.BlockSpec((d, tn), lambda i, j: (0, j)),
                  pl.BlockSpec((d, tn), lambda i, j: (0, j + nj)),
                  pl.BlockSpec((1, tn), lambda i, j: (0, j)),
                  pl.BlockSpec((1, tn), lambda i, j: (0, j + nj)),
                  pl.BlockSpec((tm, tn), lambda i, j: (i, j))],
        out_specs=pl.BlockSpec((tm, tn), lambda i, j: (i, j)),
        out_shape=jax.ShapeDtypeStruct((t, d), F32),
        compiler_params=_params(("arbitrary", "arbitrary")),
        name="glu_residual",
    )(z, w_bf, w_bf, bias, bias, x)


def _norm_kernel(x_ref, g_ref, o_ref):
    o_ref[...] = _rms(x_ref[...], g_ref[...], RMS_EPS).astype(o_ref.dtype)


def rms_norm(x, g, tm):
    t, d = x.shape
    return pl.pallas_call(
        _norm_kernel,
        grid=(t // tm,),
        in_specs=[pl.BlockSpec((tm, d), lambda i: (i, 0)),
                  pl.BlockSpec((1, d), lambda i: (0, 0))],
        out_specs=pl.BlockSpec((tm, d), lambda i: (i, 0)),
        out_shape=jax.ShapeDtypeStruct((t, d), F32),
        compiler_params=_params(("arbitrary",)),
        name="rms_norm",
    )(x, g)


def _diff_lambda(lq1, lk1, lq2, lk2, lam_init):
    return (jnp.exp(jnp.sum(lq1[...] * lk1[...], axis=-1, keepdims=True))
            - jnp.exp(jnp.sum(lq2[...] * lk2[...], axis=-1, keepdims=True)) + lam_init)


def _attn_prompt_kernel(q_ref, k_ref, v_ref, lq1, lk1, lq2, lk2, subln_ref, o_ref,
                        vtb, acc_sc, *, tq, lam_init):
    seq = q_ref.shape[0]
    scale = HEAD_DIM ** -0.5
    for j in range(seq // tq):
        vtb[j] = v_ref[j * tq:(j + 1) * tq, :].T.astype(BF16)
    lam = _diff_lambda(lq1, lk1, lq2, lk2, lam_init)
    krow = lax.broadcasted_iota(jnp.int32, (tq, tq), 0)
    qcol = lax.broadcasted_iota(jnp.int32, (tq, tq), 1)
    causal = krow <= qcol

    def q_body(qi, carry):
        q0 = pl.multiple_of(qi * tq, tq)
        qt = q_ref[pl.ds(q0, tq), :]
        acc_sc[...] = jnp.zeros(acc_sc.shape, F32)

        def kv_step(kj, masked, st):
            k0 = pl.multiple_of(kj * tq, tq)
            kt = k_ref[pl.ds(k0, tq), :]
            vt = vtb[kj]
            out = []
            for s in range(2):
                ls = slice(s * HEAD_DIM, (s + 1) * HEAD_DIM)
                sc = lax.dot_general(kt[:, ls], qt[:, ls], (_NT, ((), ())), preferred_element_type=F32) * scale
                if masked:
                    sc = jnp.where(causal, sc, -jnp.inf)
                m_prev, l_prev = st[2 * s], st[2 * s + 1]
                m_new = jnp.maximum(m_prev, jnp.max(sc, axis=0, keepdims=True))
                alpha = jnp.exp(m_prev - m_new)
                p = jnp.exp(sc - m_new)
                out += [m_new, alpha * l_prev + jnp.sum(p, axis=0, keepdims=True)]
                acc_sc[s] = alpha * acc_sc[s] + jnp.dot(vt, p.astype(BF16), preferred_element_type=F32)
            return tuple(out)

        ninf = jnp.full((1, tq), -jnp.inf, F32)
        zero = jnp.zeros((1, tq), F32)
        st = lax.fori_loop(0, qi, lambda kj, st: kv_step(kj, False, st), (ninf, zero, ninf, zero))
        _, l0, _, l1 = kv_step(qi, True, st)
        ot = acc_sc[0] / l0 - lam * (acc_sc[1] / l1)
        ot = ot * lax.rsqrt(jnp.mean(ot * ot, axis=0, keepdims=True) + SUBLN_EPS)
        o_ref[pl.ds(q0, tq), :] = (ot.T * subln_ref[...] * (1.0 - lam_init)).astype(o_ref.dtype)
        return carry

    lax.fori_loop(0, seq // tq, q_body, 0)


def attn_prompt(q, k, v, lams, subln, batch, seq, lam_init, tq=256):
    t, d = q.shape
    blk = pl.BlockSpec((seq, HEAD_W), lambda b, h: (b, h))
    vec = pl.BlockSpec((1, HEAD_DIM), lambda b, h: (0, 0))
    return pl.pallas_call(
        functools.partial(_attn_prompt_kernel, tq=tq, lam_init=lam_init),
        grid=(batch, N_HEADS),
        in_specs=[blk, blk, blk, vec, vec, vec, vec,
                  pl.BlockSpec((1, HEAD_W), lambda b, h: (0, 0))],
        out_specs=blk,
        out_shape=jax.ShapeDtypeStruct((t, d), BF16),
        scratch_shapes=[pltpu.VMEM((seq // tq, HEAD_W, tq), BF16), pltpu.VMEM((2, HEAD_W, tq), F32)],
        compiler_params=_params(("arbitrary", "arbitrary")),
        name="attn_prompt",
    )(q, k, v, *lams, subln)


def _attn_sample_kernel(pt_ref, q_ref, kn_ref, vn_ref, *rest, n_new, n_pp, lam_init):
    del pt_ref
    kp_refs, vp_refs = rest[:n_pp], rest[n_pp:2 * n_pp]
    bias_ref, biasn_ref, lq1, lk1, lq2, lk2, subln_ref, o_ref, pbuf, m_sc, l_sc, acc_sc = rest[2 * n_pp:]
    p = pl.program_id(1)
    scale = HEAD_DIM ** -0.5
    qrows = q_ref[...].astype(BF16)

    @pl.when(p == 0)
    def _():
        m_sc[...] = jnp.full(m_sc.shape, -jnp.inf, F32)
        l_sc[...] = jnp.zeros(l_sc.shape, F32)
        acc_sc[...] = jnp.zeros(acc_sc.shape, F32)

    def attend(k_rows, v_rows, bias):
        r = k_rows.shape[0]
        st = lax.dot_general(k_rows.astype(BF16), qrows, (_NT, ((), ())), preferred_element_type=F32) * scale
        st = st.reshape(r // bias.shape[0], bias.shape[0], LANES) + bias[None]
        m_prev = m_sc[...]
        m_new = jnp.maximum(m_prev, jnp.max(jnp.max(st, axis=0), axis=0, keepdims=True))
        alpha = jnp.exp(m_prev - m_new)
        pr = jnp.exp(st - m_new)
        l_sc[...] = alpha * l_sc[...] + jnp.sum(jnp.sum(pr, axis=0), axis=0, keepdims=True)
        m_sc[...] = m_new
        pbuf[0:r, :] = pr.reshape(r, LANES)
        pv = pbuf[pl.ds(0, r // 2, stride=2), :] + pbuf[pl.ds(1, r // 2, stride=2), :]
        vt = v_rows.T.astype(BF16)
        acc_sc[...] = alpha * acc_sc[...] + jnp.dot(vt, pv.astype(BF16), preferred_element_type=F32)

    for kp_ref, vp_ref in zip(kp_refs, vp_refs):
        attend(kp_ref[...], vp_ref[...], bias_ref[...])

    @pl.when(p == pl.num_programs(1) - 1)
    def _():
        attend(kn_ref[...], vn_ref[...], biasn_ref[...])
        lam = _diff_lambda(lq1, lk1, lq2, lk2, lam_init)
        on = acc_sc[...] / l_sc[...]
        d = on - lam * pltpu.roll(on, LANES - n_new, 1)
        d = d * lax.rsqrt(jnp.mean(d * d, axis=0, keepdims=True) + SUBLN_EPS)
        o = d.T * subln_ref[...] * (1.0 - lam_init)
        for h in range(N_HEADS):
            o_ref[:, h * HEAD_W:(h + 1) * HEAD_W] = o[h * 2 * n_new:h * 2 * n_new + n_new]


def attn_sample(q, k_new, v_new, cache_k, cache_v, page_table, lams, subln, layer, lam_init, n_pp=4):
    b, n_new, d = q.shape
    n_pages = page_table.shape[1]
    n_phys = cache_k.shape[1]
    kh = 2 * N_HEADS
    nrow = N_HEADS * 2 * n_new
    n_slot = 16
    assert nrow <= LANES and n_new <= n_slot
    k_view = cache_k.reshape(cache_k.shape[0] * n_phys, PAGE_SIZE * kh, HEAD_DIM)
    v_view = cache_v.reshape(cache_v.shape[0] * n_phys, PAGE_SIZE * N_HEADS, HEAD_W)
    qrows = q.reshape(b, n_new, kh, HEAD_DIM).transpose(0, 2, 1, 3).reshape(b, nrow, HEAD_DIM)
    qrows = jnp.pad(qrows, ((0, 0), (0, LANES - nrow), (0, 0)))
    kn = jnp.pad(k_new, ((0, 0), (0, (n_slot - n_new) * kh), (0, 0)))
    vn = jnp.pad(v_new, ((0, 0), (0, n_slot - n_new), (0, 0))).reshape(b, n_slot * N_HEADS, HEAD_W)
    lane = jnp.arange(LANES)
    own = (jnp.arange(kh)[:, None] == (lane // n_new)[None, :]) | (lane >= nrow)[None, :]
    bias = jnp.where(own, 0.0, -jnp.inf).astype(F32)
    tokn = jnp.arange(n_slot)[:, None, None]
    seen = (tokn < n_new) & (tokn <= (lane % n_new)[None, None, :])
    bias_new = jnp.where((own[None] & seen) | (lane >= nrow)[None, None, :], 0.0, -jnp.inf)
    bias_new = bias_new.astype(F32).reshape(n_slot * kh, LANES)
    tok = pl.BlockSpec((None, n_new, d), lambda i, p, pt: (i, 0, 0))

    def per_seq(rows, width):
        return pl.BlockSpec((None, rows, width), lambda i, p, pt: (i, 0, 0))

    def page_spec(rows, width, k):
        return pl.BlockSpec((None, rows, width),
                            lambda i, p, pt: (layer * n_phys + pt[i * n_pages + p * n_pp + k], 0, 0))

    def const(rows, width):
        return pl.BlockSpec((rows, width), lambda i, p, pt: (0, 0))

    grid_spec = pltpu.PrefetchScalarGridSpec(
        num_scalar_prefetch=1,
        grid=(b, n_pages // n_pp),
        in_specs=[per_seq(LANES, HEAD_DIM), per_seq(n_slot * kh, HEAD_DIM), per_seq(n_slot * N_HEADS, HEAD_W)]
                 + [page_spec(PAGE_SIZE * kh, HEAD_DIM, k) for k in range(n_pp)]
                 + [page_spec(PAGE_SIZE * N_HEADS, HEAD_W, k) for k in range(n_pp)]
                 + [const(kh, LANES), const(n_slot * kh, LANES)]
                 + [const(1, HEAD_DIM)] * 4 + [const(1, HEAD_W)],
        out_specs=tok,
        scratch_shapes=[pltpu.VMEM((PAGE_SIZE * kh, LANES), F32), pltpu.VMEM((1, LANES), F32),
                        pltpu.VMEM((1, LANES), F32), pltpu.VMEM((HEAD_W, LANES), F32)])
    return pl.pallas_call(
        functools.partial(_attn_sample_kernel, n_new=n_new, n_pp=n_pp, lam_init=lam_init),
        grid_spec=grid_spec,
        out_shape=jax.ShapeDtypeStruct((b, n_new, d), F32),
        compiler_params=_params(("arbitrary", "arbitrary")),
        name="attn_sample",
    )(page_table.reshape(-1), qrows, kn, vn, *([k_view] * n_pp), *([v_view] * n_pp), bias, bias_new, *lams, subln)


def _router_kernel(xp_ref, xs_ref, g_ref, wh_ref, wl_ref, b_ref, tri_ref, xn_ref, ids_ref, gates_ref, cnt_ref,
                   run_sc, *, n_p):
    i = pl.program_id(0)

    @pl.when(i == 0)
    def _():
        run_sc[...] = jnp.zeros(run_sc.shape, F32)

    x = jnp.where(i < n_p, xp_ref[...], xs_ref[...])
    xn = _rms(x, g_ref[...], RMS_EPS)
    xn_ref[...] = xn
    xh, xl = _split3(xn)
    d = functools.partial(jnp.dot, preferred_element_type=F32)
    logits = d(xh, wh_ref[...]) + d(xl, wh_ref[...]) + d(xh, wl_ref[...]) + b_ref[...]
    lane = lax.broadcasted_iota(jnp.int32, logits.shape, 1)
    ninf = -jnp.inf
    gl = jnp.where(lane < N_EXPERT_GROUPS, logits, ninf)
    gmax = jnp.max(gl, axis=-1, keepdims=True)
    g_val = 1.0 / jnp.sum(jnp.exp(gl - gmax), axis=-1, keepdims=True)
    g_idx = jnp.min(jnp.where(gl == gmax, lane, LANES), axis=-1, keepdims=True)
    lo = N_EXPERT_GROUPS + EXPERTS_PER_GROUP * g_idx
    el = jnp.where((lane >= lo) & (lane < lo + EXPERTS_PER_GROUP), logits, ninf)
    e1 = jnp.max(el, axis=-1, keepdims=True)
    i1 = jnp.min(jnp.where(el == e1, lane, LANES), axis=-1, keepdims=True)
    el2 = jnp.where(lane == i1, ninf, el)
    e2 = jnp.max(el2, axis=-1, keepdims=True)
    i2 = jnp.min(jnp.where(el2 == e2, lane, LANES), axis=-1, keepdims=True)
    r = jnp.exp(e2 - e1)
    w1 = g_val / (1.0 + r)
    w2 = g_val * r / (1.0 + r)
    chose = jnp.where(lane == i1, 1.0, jnp.where(lane == i2, 1.0, 0.0))
    before = jnp.dot(tri_ref[...], chose.astype(BF16), preferred_element_type=F32) + run_sc[...]
    r1 = jnp.sum(jnp.where(lane == i1, before, 0.0), axis=-1, keepdims=True).astype(jnp.int32)
    r2 = jnp.sum(jnp.where(lane == i2, before, 0.0), axis=-1, keepdims=True).astype(jnp.int32)
    run_sc[...] = run_sc[...] + jnp.sum(chose, axis=0, keepdims=True)
    ids_ref[...] = jnp.where(lane == 0, i1 - N_EXPERT_GROUPS, jnp.where(lane == 1, i2 - N_EXPERT_GROUPS,
                             jnp.where(lane == 2, r1, jnp.where(lane == 3, r2, 0))))
    gates_ref[...] = jnp.where(lane == 0, w1, jnp.where(lane == 1, w2, 0.0))

    @pl.when(i == pl.num_programs(0) - 1)
    def _():
        cnt_ref[...] = run_sc[...]


def moe_route(xp, xs, g, w_hi, w_lo, bias, tm):
    (tp, d), ts = xp.shape, xs.shape[0]
    n_p, n_s = tp // tm, ts // tm
    t = tp + ts
    row = lambda i: (i, 0)
    const = lambda i: (0, 0)
    tri = jnp.tril(jnp.ones((tm, tm), BF16), -1)
    return pl.pallas_call(
        functools.partial(_router_kernel, n_p=n_p),
        grid=(n_p + n_s,),
        in_specs=[pl.BlockSpec((tm, d), lambda i: (jnp.minimum(i, n_p - 1), 0)),
                  pl.BlockSpec((tm, d), lambda i: (jnp.maximum(i - n_p, 0), 0)),
                  pl.BlockSpec((1, d), const),
                  pl.BlockSpec((d, LANES), const), pl.BlockSpec((d, LANES), const),
                  pl.BlockSpec((1, LANES), const), pl.BlockSpec((tm, tm), const)],
        out_specs=[pl.BlockSpec((tm, d), row), pl.BlockSpec((tm, LANES), row), pl.BlockSpec((tm, LANES), row),
                   pl.BlockSpec((1, LANES), const)],
        out_shape=[jax.ShapeDtypeStruct((t, d), F32), jax.ShapeDtypeStruct((t, LANES), jnp.int32),
                   jax.ShapeDtypeStruct((t, LANES), F32), jax.ShapeDtypeStruct((1, LANES), F32)],
        scratch_shapes=[pltpu.VMEM((1, LANES), F32)],
        compiler_params=_params(("arbitrary",)),
        name="moe_route",
    )(xp, xs, g, w_hi, w_lo, bias, tri)


def _row_gather_start(idx_ref, base, n, src_hbm, dst, sem):
    def body(r, c):
        pltpu.make_async_copy(src_hbm.at[pl.ds(idx_ref[base + r], 1), :], dst.at[pl.ds(r, 1), :], sem).start()
        return c
    lax.fori_loop(0, n, body, 0, unroll=8)


def _row_gather_wait(dst, sem):
    pltpu.make_async_copy(dst, dst, sem).wait()


def _expert_kernel(bexp_ref, rtok_ref, nused_ref, xn_hbm, w13_ref, w2_ref, ys_ref, xbuf, sem, *, tm):
    del bexp_ref
    rb = pl.program_id(0)
    nused = nused_ref[0]

    @pl.when(rb == 0)
    def _():
        _row_gather_start(rtok_ref, 0, tm, xn_hbm, xbuf.at[0], sem.at[0])

    @pl.when(rb + 1 < nused)
    def _():
        nxt = (rb + 1) % 2
        _row_gather_start(rtok_ref, (rb + 1) * tm, tm, xn_hbm, xbuf.at[nxt], sem.at[nxt])

    @pl.when(rb < nused)
    def _():
        slot = rb % 2
        _row_gather_wait(xbuf.at[slot], sem.at[slot])
        x = xbuf[slot].astype(BF16)
        h = jnp.dot(x, w13_ref[...], preferred_element_type=F32)
        ga, up = h[:, :EXPERT_FF], h[:, EXPERT_FF:]
        a = ga * (1.0 / (1.0 + jnp.exp(-ga))) * up
        ys_ref[...] = jnp.dot(a.astype(BF16), w2_ref[...], preferred_element_type=F32)

    @pl.when(rb >= nused)
    def _():
        ys_ref[...] = jnp.zeros(ys_ref.shape, F32)


def moe_experts(xn, w13_bf, w2_bf, block_exp, row_tok, n_used, tm):
    t, d = xn.shape
    n_blocks = block_exp.shape[0]
    grid_spec = pltpu.PrefetchScalarGridSpec(
        num_scalar_prefetch=3,
        grid=(n_blocks,),
        in_specs=[pl.BlockSpec(memory_space=pl.ANY),
                  pl.BlockSpec((None, d, 2 * EXPERT_FF), lambda rb, be, rt, nu: (be[rb], 0, 0)),
                  pl.BlockSpec((None, EXPERT_FF, d), lambda rb, be, rt, nu: (be[rb], 0, 0))],
        out_specs=pl.BlockSpec((tm, d), lambda rb, be, rt, nu: (rb, 0)),
        scratch_shapes=[pltpu.VMEM((2, tm, d), F32), pltpu.SemaphoreType.DMA((2,))])
    return pl.pallas_call(
        functools.partial(_expert_kernel, tm=tm),
        grid_spec=grid_spec,
        out_shape=jax.ShapeDtypeStruct((n_blocks * tm, d), F32),
        compiler_params=_params(("arbitrary",)),
        name="moe_experts",
    )(block_exp, row_tok, n_used, xn, w13_bf, w2_bf)


def _combine_kernel(pos_ref, ys_hbm, x_ref, gates_ref, o_ref, ybuf, sem, *, tm):
    i = pl.program_id(0)
    n = pl.num_programs(0)

    def start(tile, slot):
        _row_gather_start(pos_ref, tile * 2 * tm, 2 * tm, ys_hbm, ybuf.at[slot], sem.at[slot])

    @pl.when(i == 0)
    def _():
        start(0, 0)

    @pl.when(i + 1 < n)
    def _():
        start(i + 1, (i + 1) % 2)

    slot = i % 2
    _row_gather_wait(ybuf.at[slot], sem.at[slot])
    g = gates_ref[...]
    o_ref[...] = x_ref[...] + (g[:, 0:1] * ybuf[slot, :tm] + g[:, 1:2] * ybuf[slot, tm:])


def moe_combine(ys, x, gates, pos, tm):
    t, d = x.shape
    pos_tiles = pos.reshape(t // tm, tm, 2).transpose(0, 2, 1).reshape(-1)
    grid_spec = pltpu.PrefetchScalarGridSpec(
        num_scalar_prefetch=1,
        grid=(t // tm,),
        in_specs=[pl.BlockSpec(memory_space=pl.ANY),
                  pl.BlockSpec((tm, d), lambda i, ps: (i, 0)),
                  pl.BlockSpec((tm, LANES), lambda i, ps: (i, 0))],
        out_specs=pl.BlockSpec((tm, d), lambda i, ps: (i, 0)),
        scratch_shapes=[pltpu.VMEM((2, 2 * tm, d), F32), pltpu.SemaphoreType.DMA((2,))])
    return pl.pallas_call(
        functools.partial(_combine_kernel, tm=tm),
        grid_spec=grid_spec,
        out_shape=jax.ShapeDtypeStruct((t, d), F32),
        compiler_params=_params(("arbitrary",)),
        name="moe_combine",
    )(pos_tiles, ys, x, gates)


def hier_moe(xp, xs, g, w_group, b_group, w_expert, b_expert, w13_bf, w2_bf, tm_blk=256, tm_tok=128):
    tp, d = xp.shape
    pad = LANES - N_EXPERT_GROUPS - N_EXPERTS
    w_cat = jnp.concatenate([w_group, w_expert, jnp.zeros((d, pad), F32)], axis=1)
    b_cat = jnp.concatenate([b_group, b_expert, jnp.zeros((pad,), F32)])[None]
    w_hi = w_cat.astype(BF16)
    w_lo = (w_cat - w_hi.astype(F32)).astype(BF16)
    xn, ids, gates, cnt = moe_route(xp, xs, g, w_hi, w_lo, b_cat, tm_tok)
    t = xn.shape[0]
    n_assign = 2 * t
    counts = cnt[0, N_EXPERT_GROUPS:N_EXPERT_GROUPS + N_EXPERTS].astype(jnp.int32)
    padded = ((counts + tm_blk - 1) // tm_blk) * tm_blk
    pad_end = jnp.cumsum(padded)
    pad_start = pad_end - padded
    pos = pad_start[ids[:, :2]] + ids[:, 2:4]
    n_blocks = -(-n_assign // tm_blk) + N_EXPERTS
    flat_t = jnp.arange(n_assign, dtype=jnp.int32) // 2
    row_tok = jnp.zeros((n_blocks * tm_blk,), jnp.int32).at[pos.reshape(n_assign)].set(flat_t)
    blk_start = jnp.arange(n_blocks, dtype=jnp.int32) * tm_blk
    block_exp = jnp.minimum(jnp.sum((pad_end[None, :] <= blk_start[:, None]).astype(jnp.int32), axis=1), N_EXPERTS - 1)
    n_used = (pad_end[-1:] // tm_blk).astype(jnp.int32)
    ys = moe_experts(xn, w13_bf, w2_bf, block_exp, row_tok, n_used, tm_blk)
    return (moe_combine(ys, xp, gates[:tp], pos[:tp], tm_tok),
            moe_combine(ys, xs, gates[tp:], pos[tp:], min(tm_tok, xs.shape[0])))


def _cmul(ar, ai, br, bi):
    return ar * br - ai * bi, ar * bi + ai * br


def _ssm_prep_kernel(are_ref, aim_ref, ldt_ref, bre_ref, bim_ref, cre_ref, cim_ref,
                     m_ref, bc_ref, cc_ref, a16_ref, a4_ref, *, n_new):
    p = SSM_STATE
    nc = SSM_CHUNK
    a_re, a_im = are_ref[...], aim_ref[...]
    dt = jnp.exp(ldt_ref[...])
    mag = jnp.exp(dt * a_re)
    ab_re = mag * jnp.cos(dt * a_im)
    ab_im = mag * jnp.sin(dt * a_im)
    den = a_re * a_re + a_im * a_im
    nr = ab_re - 1.0
    fr = (nr * a_re + ab_im * a_im) / den
    fi = (ab_im * a_re - nr * a_im) / den
    bt_re, bt_im = bre_ref[...], bim_ref[...]
    bb_re = fr * bt_re - fi * bt_im
    bb_im = fr * bt_im + fi * bt_re
    c_re, c_im = cre_ref[...], cim_ref[...]
    pw = [(jnp.ones_like(ab_re), jnp.zeros_like(ab_re))]
    for _ in range(nc):
        pw.append(_cmul(pw[-1][0], pw[-1][1], ab_re, ab_im))
    wst = []
    for tau in range(nc + 1):
        wr, wi = _cmul(c_re, c_im, pw[tau][0], pw[tau][1])
        wst.append(jnp.concatenate([wr, -wi], axis=1))
    wstack = jnp.concatenate(wst[:nc], axis=0)
    bs_t = jnp.concatenate([bb_re, bb_im], axis=1)
    q = _dot3(bs_t, wstack, _NT)
    qpad = jnp.concatenate([jnp.zeros_like(q), q], axis=1)
    w = nc * SSM_GROUP_CH
    rows = []
    for s in range(nc):
        off = (nc - s) * SSM_GROUP_CH
        rows.append(qpad[:, off:off + w])
    m_ref[...] = jnp.concatenate(rows, axis=0).astype(m_ref.dtype)
    bc = []
    for s in range(nc):
        xr, xi = _cmul(bb_re, bb_im, pw[nc - 1 - s][0], pw[nc - 1 - s][1])
        bc.append(jnp.concatenate([xr, xi], axis=1))
    bc_ref[...] = jnp.concatenate(bc, axis=0).astype(bc_ref.dtype)
    cc_ref[...] = jnp.concatenate(wst[1:], axis=0).astype(cc_ref.dtype)
    a16_ref[...] = jnp.concatenate([pw[nc][0], pw[nc][1]], axis=1)
    a4_ref[...] = jnp.concatenate([pw[n_new][0], pw[n_new][1]], axis=1)


def ssm_prep(a_re, a_im, log_dt, b_re, b_im, c_re, c_im, n_new):
    g, p = a_re.shape
    gc = SSM_GROUP_CH
    w = SSM_CHUNK * gc
    vec = pl.BlockSpec((None, 1, p), lambda i: (i, 0, 0))
    mat = pl.BlockSpec((None, gc, p), lambda i: (i, 0, 0))
    bt_re = b_re.transpose(0, 2, 1)
    bt_im = b_im.transpose(0, 2, 1)
    return pl.pallas_call(
        functools.partial(_ssm_prep_kernel, n_new=n_new),
        grid=(g,),
        in_specs=[vec, vec, pl.BlockSpec((None, 1, 1), lambda i: (i, 0, 0)), mat, mat, mat, mat],
        out_specs=[pl.BlockSpec((None, w, w), lambda i: (i, 0, 0)),
                   pl.BlockSpec((None, w, 2 * p), lambda i: (i, 0, 0)),
                   pl.BlockSpec((None, w, 2 * p), lambda i: (i, 0, 0)),
                   pl.BlockSpec((None, 1, 2 * p), lambda i: (i, 0, 0)),
                   pl.BlockSpec((None, 1, 2 * p), lambda i: (i, 0, 0))],
        out_shape=[jax.ShapeDtypeStruct((g, w, w), BF16),
                   jax.ShapeDtypeStruct((g, w, 2 * p), BF16),
                   jax.ShapeDtypeStruct((g, w, 2 * p), BF16),
                   jax.ShapeDtypeStruct((g, 1, 2 * p), F32),
                   jax.ShapeDtypeStruct((g, 1, 2 * p), F32)],
        compiler_params=_params(("arbitrary",)),
        name="ssm_prep",
    )(a_re[:, None], a_im[:, None], log_dt[:, None, None], bt_re, bt_im, c_re, c_im)


GROUPS_PER_VREG = LANES // SSM_GROUP_CH


def _chunk_perm():
    n = GROUPS_PER_VREG
    idx = np.arange(n * LANES)
    a, b, c = idx // LANES, (idx // SSM_GROUP_CH) % n, idx % SSM_GROUP_CH
    m = np.zeros((n * LANES, n * LANES), np.float32)
    m[idx, (b * n + a) * SSM_GROUP_CH + c] = 1.0
    return jnp.asarray(m, BF16)


def _ssm_in_kernel(x_ref, g_ref, p_ref, u_ref, scr):
    tm, d = x_ref.shape
    nc, nj, n = tm // SSM_CHUNK, d // LANES, GROUPS_PER_VREG
    xn = _rms(x_ref[...], g_ref[...], RMS_EPS)
    for j in range(nj):
        scr[pl.ds(j, tm, stride=nj), :] = xn[:, j * LANES:(j + 1) * LANES]
    perm = p_ref[...]
    for j in range(nj):
        for h in range(SSM_CHUNK // n):
            lhs = jnp.concatenate([scr[pl.ds((h * n + sl) * nj + j, nc, stride=SSM_CHUNK * nj), :]
                                   for sl in range(n)], axis=1).astype(BF16)
            r = jnp.dot(lhs, perm, preferred_element_type=F32)
            for gl in range(n):
                u_ref[j * n + gl, :, h * LANES:(h + 1) * LANES] = r[:, gl * LANES:(gl + 1) * LANES].astype(BF16)


def ssm_chunk_in(x, g, perm, tm):
    t, d = x.shape
    ng, w = d // SSM_GROUP_CH, SSM_CHUNK * SSM_GROUP_CH
    nc = tm // SSM_CHUNK
    return pl.pallas_call(
        _ssm_in_kernel,
        grid=(t // tm,),
        in_specs=[pl.BlockSpec((tm, d), lambda i: (i, 0)), pl.BlockSpec((1, d), lambda i: (0, 0)),
                  pl.BlockSpec(perm.shape, lambda i: (0, 0))],
        out_specs=pl.BlockSpec((ng, nc, w), lambda i: (0, i, 0)),
        out_shape=jax.ShapeDtypeStruct((ng, t // SSM_CHUNK, w), BF16),
        scratch_shapes=[pltpu.VMEM((tm * (d // LANES), LANES), F32)],
        compiler_params=_params(("arbitrary",)),
        name="ssm_chunk_in",
    )(x, g, perm)


def _ssm_out_kernel(z_ref, p_ref, o_ref, scr):
    tm, d = o_ref.shape
    nc, nj, n = tm // SSM_CHUNK, d // LANES, GROUPS_PER_VREG
    perm = p_ref[...]
    for j in range(nj):
        for h in range(SSM_CHUNK // n):
            lhs = jnp.concatenate([z_ref[j * n + gl, :, h * LANES:(h + 1) * LANES] for gl in range(n)], axis=1)
            r = jnp.dot(lhs, perm, preferred_element_type=F32)
            for tl in range(n):
                scr[pl.ds((h * n + tl) * nj + j, nc, stride=SSM_CHUNK * nj), :] = r[:, tl * LANES:(tl + 1) * LANES]
    for j in range(nj):
        o_ref[:, j * LANES:(j + 1) * LANES] = scr[pl.ds(j, tm, stride=nj), :].astype(o_ref.dtype)


def ssm_chunk_out(z, perm, tm, d):
    ng, n_c, w = z.shape
    nc = tm // SSM_CHUNK
    return pl.pallas_call(
        _ssm_out_kernel,
        grid=(n_c // nc,),
        in_specs=[pl.BlockSpec((ng, nc, w), lambda i: (0, i, 0)), pl.BlockSpec(perm.shape, lambda i: (0, 0))],
        out_specs=pl.BlockSpec((tm, d), lambda i: (i, 0)),
        out_shape=jax.ShapeDtypeStruct((n_c * SSM_CHUNK, d), z.dtype),
        scratch_shapes=[pltpu.VMEM((tm * (d // LANES), LANES), F32)],
        compiler_params=_params(("arbitrary",)),
        name="ssm_chunk_out",
    )(z, perm)


def _ssm_x_kernel(u_ref, bc_ref, x_ref, *, k):
    for gi in range(u_ref.shape[0]):
        bc = bc_ref[gi]
        x_ref[gi] = jnp.dot(u_ref[gi, :, :k], bc[bc.shape[0] - k:], preferred_element_type=F32)


def _ssm_scan_kernel(xp_ref, xs_ref, h0_ref, a16_ref, a4_ref, hs_ref, hfin_ref, hsfin_ref, *, n_chunks, batch):
    p = SSM_STATE

    def coef(a):
        ar, ai = a[:, :, :p], a[:, :, p:]
        return jnp.concatenate([ar, ar], axis=-1), jnp.concatenate([-ai, ai], axis=-1)

    def step(h, c1, c2):
        return c1 * h + c2 * pltpu.roll(h, p, 2)

    c1, c2 = coef(a16_ref[...])

    def body(c, h):
        rows = pl.ds(c, batch, stride=n_chunks)
        hs_ref[:, rows, :] = h
        return step(h, c1, c2) + xp_ref[:, rows, :]

    h0 = jnp.zeros((xp_ref.shape[0], batch, 2 * p), F32)
    hfin_ref[...] = lax.fori_loop(0, n_chunks, body, h0)
    d1, d2 = coef(a4_ref[...])
    hsfin_ref[...] = step(h0_ref[...], d1, d2) + xs_ref[...]


def _ssm_y_kernel(u_ref, hs_ref, m_ref, cc_ref, d_ref, z_ref):
    c = math.sqrt(2.0 / math.pi)
    for gi in range(u_ref.shape[0]):
        u = u_ref[gi]
        y = jnp.dot(u, m_ref[gi], preferred_element_type=F32)
        y = y + lax.dot_general(hs_ref[gi].astype(BF16), cc_ref[gi], (_NT, ((), ())), preferred_element_type=F32)
        y = y + d_ref[gi] * u.astype(F32)
        z = 0.5 * y * (1.0 + jnp.tanh(c * (y + 0.044715 * (y * y * y))))
        z_ref[gi] = z.astype(z_ref.dtype)


def ssm_mix(up, us, h0_g, ops, d_tile, n_chunks, batch, n_new, g_blk=8, g_scan=16):
    m_g, bc_g, cc_g, a16, a4 = ops
    g, _, w = up.shape
    p2 = 2 * SSM_STATE
    n_s = us.shape[1]

    def blk(gb, *shape):
        return pl.BlockSpec((gb,) + shape, lambda i: (i,) + (0,) * len(shape))

    def chunk_state(u, k):
        r = u.shape[1]
        return pl.pallas_call(
            functools.partial(_ssm_x_kernel, k=k),
            grid=(g // g_blk,),
            in_specs=[blk(g_blk, r, w), blk(g_blk, w, p2)],
            out_specs=blk(g_blk, r, p2),
            out_shape=jax.ShapeDtypeStruct((g, r, p2), F32),
            compiler_params=_params(("arbitrary",)),
            name="ssm_chunk_state",
        )(u, bc_g)

    def output(u, hs):
        r = u.shape[1]
        return pl.pallas_call(
            _ssm_y_kernel,
            grid=(g // g_blk,),
            in_specs=[blk(g_blk, r, w), blk(g_blk, r, p2), blk(g_blk, w, w), blk(g_blk, w, p2), blk(g_blk, 1, w)],
            out_specs=blk(g_blk, r, w),
            out_shape=jax.ShapeDtypeStruct((g, r, w), BF16),
            compiler_params=_params(("arbitrary",)),
            name="ssm_output",
        )(u, hs, m_g, cc_g, d_tile)

    xp = chunk_state(up, w)
    xs = chunk_state(us, n_new * SSM_GROUP_CH)
    r = up.shape[1]
    hs, hfin, hsfin = pl.pallas_call(
        functools.partial(_ssm_scan_kernel, n_chunks=n_chunks, batch=batch),
        grid=(g // g_scan,),
        in_specs=[blk(g_scan, r, p2), blk(g_scan, n_s, p2), blk(g_scan, n_s, p2), blk(g_scan, 1, p2),
                  blk(g_scan, 1, p2)],
        out_specs=[blk(g_scan, r, p2), blk(g_scan, batch, p2), blk(g_scan, n_s, p2)],
        out_shape=[jax.ShapeDtypeStruct((g, r, p2), F32),
                   jax.ShapeDtypeStruct((g, batch, p2), F32),
                   jax.ShapeDtypeStruct((g, n_s, p2), F32)],
        compiler_params=_params(("arbitrary",)),
        name="ssm_scan",
    )(xp, xs, h0_g, a16, a4)
    return output(up, hs), output(us, h0_g), hfin, hsfin


def kernel(x_prompt, x_sample, cache_k, cache_v, state_ssm_re, state_ssm_im, page_table, norm_mix, norm_ffn, norm_final, attn_w_qkv, attn_lambda_q1, attn_lambda_k1, attn_lambda_q2, attn_lambda_k2, attn_subln, attn_w_o, ssm_A_re, ssm_A_im, ssm_log_dt, ssm_B_re, ssm_B_im, ssm_C_re, ssm_C_im, ssm_D, ssm_glu_w, ssm_glu_b, moe_w_group, moe_b_group, moe_w_expert, moe_b_expert, moe_w13, moe_w2):
    batch, seq, d = x_prompt.shape
    dec_b, dec_s, _ = x_sample.shape
    depth = norm_mix.shape[0]
    tp, ts = batch * seq, dec_b * dec_s
    xp = x_prompt.reshape(tp, d)
    xs = x_sample.reshape(ts, d)
    g, pst, gc = ssm_B_re.shape[1:]
    n_chunks = seq // SSM_CHUNK
    k_p, v_p, k_s, v_s = [], [], [], []
    hr_p, hi_p, hr_s, hi_s = [], [], [], []
    for i in range(depth):
        gm = norm_mix[i][None]
        if i % 2 == 0:
            a = i // 2
            lam_init = 0.8 - 0.6 * math.exp(-0.3 * i)
            lams = (attn_lambda_q1[a][None], attn_lambda_k1[a][None], attn_lambda_q2[a][None], attn_lambda_k2[a][None])
            subln = attn_subln[a][None]
            w_qkv = attn_w_qkv[a].astype(BF16)
            w_o = attn_w_o[a].astype(BF16)
            qp, khp, kp, vp = qkv_proj(xp, gm, w_qkv, 512, 512)
            qs, _, ks, vs = qkv_proj(xs, gm, w_qkv, ts, 512)
            op = attn_prompt(qp, khp, vp, lams, subln, batch, seq, lam_init)
            os_ = attn_sample(qs.reshape(dec_b, dec_s, d), ks.reshape(dec_b, dec_s * 2 * N_HEADS, HEAD_DIM),
                              vs.reshape(dec_b, dec_s, d), cache_k, cache_v, page_table, lams, subln, a, lam_init)
            xp = matmul_residual(op, w_o, xp, 1024, 512)
            xs = matmul_residual(os_.reshape(ts, d), w_o, xs, ts, 512)
            k_p.append(kp.reshape(batch, seq, N_HEADS, 2, HEAD_DIM))
            v_p.append(vp.reshape(batch, seq, N_HEADS, HEAD_W))
            k_s.append(ks.reshape(dec_b, dec_s, N_HEADS, 2, HEAD_DIM))
            v_s.append(vs.reshape(dec_b, dec_s, N_HEADS, HEAD_W))
        else:
            s = i // 2
            ops = ssm_prep(ssm_A_re[s], ssm_A_im[s], ssm_log_dt[s], ssm_B_re[s], ssm_B_im[s],
                           ssm_C_re[s], ssm_C_im[s], dec_s)
            perm = _chunk_perm()
            up = ssm_chunk_in(xp, gm, perm, 1024)
            hs = rms_norm(xs, gm, ts)
            us = hs.reshape(dec_b, dec_s, g, gc).transpose(2, 0, 1, 3).reshape(g, dec_b, dec_s * gc)
            us = jnp.pad(us, ((0, 0), (0, 0), (0, (SSM_CHUNK - dec_s) * gc))).astype(BF16)
            h0_g = jnp.concatenate([state_ssm_re[s], state_ssm_im[s]], axis=-1).transpose(1, 0, 2)
            d_tile = jnp.tile(ssm_D[s], (1, SSM_CHUNK))[:, None]
            zp_g, zs_g, hfin, hsfin = ssm_mix(up, us, h0_g, ops, d_tile, n_chunks, batch, dec_s)
            zp = ssm_chunk_out(zp_g, perm, 1024, d)
            zs = zs_g[:, :, :dec_s * gc].reshape(g, dec_b, dec_s, gc).transpose(1, 2, 0, 3).reshape(ts, d)
            glu_w = ssm_glu_w[s].astype(BF16)
            glu_b = ssm_glu_b[s][None]
            xp = glu_residual(zp, glu_w, glu_b, xp, 1024, 512)
            xs = glu_residual(zs, glu_w, glu_b, xs, ts, 512)
            hfin = hfin.transpose(1, 0, 2)
            hsfin = hsfin.transpose(1, 0, 2)
            hr_p.append(hfin[..., :pst])
            hi_p.append(hfin[..., pst:])
            hr_s.append(hsfin[..., :pst])
            hi_s.append(hsfin[..., pst:])
        xp, xs = hier_moe(xp, xs, norm_ffn[i][None], moe_w_group[i], moe_b_group[i], moe_w_expert[i],
                          moe_b_expert[i], moe_w13[i].astype(BF16), moe_w2[i].astype(BF16))
    y_prompt = rms_norm(xp, norm_final[None], 1024).reshape(batch, seq, d)
    y_sample = rms_norm(xs, norm_final[None], ts).reshape(dec_b, dec_s, d)
    return (y_prompt, y_sample, jnp.stack(k_p), jnp.stack(v_p), jnp.stack(k_s), jnp.stack(v_s),
            jnp.stack(hr_p), jnp.stack(hi_p), jnp.stack(hr_s), jnp.stack(hi_s))
```

```python
import functools
import math

import numpy as np
import jax
import jax.numpy as jnp
from jax import lax
from jax.experimental import pallas as pl
from jax.experimental.pallas import tpu as pltpu

F32 = jnp.float32
BF16 = jnp.bfloat16

RMS_EPS = 1e-6
SUBLN_EPS = 1e-5
HEAD_DIM = 128
HEAD_W = 2 * HEAD_DIM
N_HEADS = 8
Q_PRESCALE = HEAD_DIM ** -0.5 * math.log2(math.e)
PAGE_SIZE = 128
N_EXPERT_GROUPS = 4
EXPERTS_PER_GROUP = 8
N_EXPERTS = N_EXPERT_GROUPS * EXPERTS_PER_GROUP
EXPERT_FF = 1024
SSM_GROUP_CH = 16
SSM_STATE = 64
SSM_CHUNK = 16
LANES = 128
VMEM_LIMIT = 56 * 1024 * 1024


def _params(sem, vmem=VMEM_LIMIT):
    return pltpu.CompilerParams(dimension_semantics=sem, vmem_limit_bytes=vmem)


def _rms(x, g, eps):
    return x * lax.rsqrt(jnp.mean(x * x, axis=-1, keepdims=True) + eps) * g


def _split3(x):
    hi = x.astype(BF16)
    lo = (x - hi.astype(F32)).astype(BF16)
    return hi, lo


def _dot3(a, b, dims):
    ah, al = _split3(a)
    bh, bl = _split3(b)
    dn = (dims, ((), ()))
    d = functools.partial(lax.dot_general, dimension_numbers=dn, preferred_element_type=F32)
    return d(ah, bh) + d(al, bh) + d(ah, bl)


_NT = ((1,), (1,))
_NN = ((1,), (0,))


def _qkv_kernel(x_ref, g_ref, w_ref, q_ref, kh_ref, kn_ref, v_ref, xn_ref, *, nj):
    j = pl.program_id(1)
    tn = w_ref.shape[1]

    @pl.when(j == 0)
    def _():
        xn_ref[...] = _rms(x_ref[...], g_ref[...], RMS_EPS).astype(BF16)

    acc = jnp.dot(xn_ref[...], w_ref[...], preferred_element_type=F32)

    @pl.when(j < nj)
    def _():
        q_ref[...] = (acc * Q_PRESCALE).astype(q_ref.dtype)

    for jj in range(nj):
        @pl.when(j == nj + jj)
        def _(jj=jj):
            kh_ref[...] = acc.astype(kh_ref.dtype)
            for i in range(tn // HEAD_DIM):
                kn_ref[:, jj * (tn // HEAD_DIM) + i, :] = acc[:, i * HEAD_DIM:(i + 1) * HEAD_DIM]

    @pl.when(j >= 2 * nj)
    def _():
        v_ref[...] = acc


def qkv_proj(x, g, w_bf, tm, tn):
    t, d = x.shape
    nj = d // tn

    def omap(part):
        return lambda i, j: (i, jnp.clip(j - part * nj, 0, nj - 1))

    return pl.pallas_call(
        functools.partial(_qkv_kernel, nj=nj),
        grid=(t // tm, 3 * nj),
        in_specs=[pl.BlockSpec((tm, d), lambda i, j: (i, 0)),
                  pl.BlockSpec((1, d), lambda i, j: (0, 0)),
                  pl.BlockSpec((d, tn), lambda i, j: (0, j))],
        out_specs=[pl.BlockSpec((tm, tn), omap(0)), pl.BlockSpec((tm, tn), omap(1)),
                   pl.BlockSpec((tm, d // HEAD_DIM, HEAD_DIM), lambda i, j: (i, 0, 0)),
                   pl.BlockSpec((tm, tn), omap(2))],
        out_shape=[jax.ShapeDtypeStruct((t, d), BF16), jax.ShapeDtypeStruct((t, d), BF16),
                   jax.ShapeDtypeStruct((t, d // HEAD_DIM, HEAD_DIM), F32), jax.ShapeDtypeStruct((t, d), F32)],
        scratch_shapes=[pltpu.VMEM((tm, d), BF16)],
        compiler_params=_params(("arbitrary", "arbitrary")),
        name="qkv_proj",
    )(x, g, w_bf)


def _mm_res_kernel(a_ref, w_ref, x_ref, o_ref):
    o_ref[...] = x_ref[...] + jnp.dot(a_ref[...].astype(BF16), w_ref[...], preferred_element_type=F32)


def matmul_residual(a, w_bf, x, tm, tn):
    t, kd = a.shape
    n = w_bf.shape[1]
    return pl.pallas_call(
        _mm_res_kernel,
        grid=(t // tm, n // tn),
        in_specs=[pl.BlockSpec((tm, kd), lambda i, j: (i, 0)),
                  pl.BlockSpec((kd, tn), lambda i, j: (0, j)),
                  pl.BlockSpec((tm, tn), lambda i, j: (i, j))],
        out_specs=pl.BlockSpec((tm, tn), lambda i, j: (i, j)),
        out_shape=jax.ShapeDtypeStruct((t, n), F32),
        compiler_params=_params(("arbitrary", "arbitrary")),
        name="matmul_residual",
    )(a, w_bf, x)


def _glu_kernel(z_ref, w1_ref, w2_ref, b1_ref, b2_ref, x_ref, o_ref):
    z = z_ref[...]
    a = jnp.dot(z, w1_ref[...], preferred_element_type=F32) + b1_ref[...]
    b = jnp.dot(z, w2_ref[...], preferred_element_type=F32) + b2_ref[...]
    o_ref[...] = x_ref[...] + a * (1.0 / (1.0 + jnp.exp(-b)))


def glu_residual(z, w_bf, bias, x, tm, tn):
    t, d = z.shape
    nj = d // tn
    return pl.pallas_call(
        _glu_kernel,
        grid=(t // tm, nj),
        in_specs=[pl.BlockSpec((tm, d), lambda i, j: (i, 0)),
                  pl.BlockSpec((d, tn), lambda i, j: (0, j)),
                  pl.BlockSpec((d, tn), lambda i, j: (0, j + nj)),
                  pl.BlockSpec((1, tn), lambda i, j: (0, j)),
                  pl.BlockSpec((1, tn), lambda i, j: (0, j + nj)),
                  pl.BlockSpec((tm, tn), lambda i, j: (i, j))],
        out_specs=pl.BlockSpec((tm, tn), lambda i, j: (i, j)),
        out_shape=jax.ShapeDtypeStruct((t, d), F32),
        compiler_params=_params(("arbitrary", "arbitrary")),
        name="glu_residual",
    )(z, w_bf, w_bf, bias, bias, x)


def _norm_kernel(x_ref, g_ref, o_ref):
    o_ref[...] = _rms(x_ref[...], g_ref[...], RMS_EPS).astype(o_ref.dtype)


def rms_norm(x, g, tm):
    t, d = x.shape
    return pl.pallas_call(
        _norm_kernel,
        grid=(t // tm,),
        in_specs=[pl.BlockSpec((tm, d), lambda i: (i, 0)),
                  pl.BlockSpec((1, d), lambda i: (0, 0))],
        out_specs=pl.BlockSpec((tm, d), lambda i: (i, 0)),
        out_shape=jax.ShapeDtypeStruct((t, d), F32),
        compiler_params=_params(("arbitrary",)),
        name="rms_norm",
    )(x, g)


def _diff_lambda(lq1, lk1, lq2, lk2, lam_init):
    return (jnp.exp(jnp.sum(lq1[...] * lk1[...], axis=-1, keepdims=True))
            - jnp.exp(jnp.sum(lq2[...] * lk2[...], axis=-1, keepdims=True)) + lam_init)


def _attn_prompt_kernel(q_ref, k_ref, v_ref, lq1, lk1, lq2, lk2, subln_ref, o_ref,
                        vtb, acc_sc, *, tq, lam_init):
    seq = q_ref.shape[0]
    for j in range(seq // tq):
        vtb[j] = v_ref[j * tq:(j + 1) * tq, :].T.astype(BF16)
    lam = _diff_lambda(lq1, lk1, lq2, lk2, lam_init)
    krow = lax.broadcasted_iota(jnp.int32, (tq, tq), 0)
    qcol = lax.broadcasted_iota(jnp.int32, (tq, tq), 1)
    causal = krow <= qcol

    def q_body(qi, carry):
        q0 = pl.multiple_of(qi * tq, tq)
        qt = q_ref[pl.ds(q0, tq), :]
        acc_sc[...] = jnp.zeros(acc_sc.shape, F32)

        def kv_step(kj, masked, st):
            k0 = pl.multiple_of(kj * tq, tq)
            kt = k_ref[pl.ds(k0, tq), :]
            vt = vtb[kj]
            out = []
            for s in range(2):
                ls = slice(s * HEAD_DIM, (s + 1) * HEAD_DIM)
                sc = lax.dot_general(kt[:, ls], qt[:, ls], (_NT, ((), ())), preferred_element_type=F32)
                if masked:
                    sc = jnp.where(causal, sc, -jnp.inf)
                m_prev, l_prev = st[2 * s], st[2 * s + 1]
                m_new = jnp.maximum(m_prev, jnp.max(sc, axis=0, keepdims=True))
                alpha = jnp.exp2(m_prev - m_new)
                p = jnp.exp2(sc - m_new)
                out += [m_new, alpha * l_prev + jnp.sum(p, axis=0, keepdims=True)]
                acc_sc[s] = alpha * acc_sc[s] + jnp.dot(vt, p.astype(BF16), preferred_element_type=F32)
            return tuple(out)

        ninf = jnp.full((1, tq), -jnp.inf, F32)
        zero = jnp.zeros((1, tq), F32)
        st = lax.fori_loop(0, qi, lambda kj, st: kv_step(kj, False, st), (ninf, zero, ninf, zero))
        _, l0, _, l1 = kv_step(qi, True, st)
        ot = acc_sc[0] / l0 - lam * (acc_sc[1] / l1)
        ot = ot * lax.rsqrt(jnp.mean(ot * ot, axis=0, keepdims=True) + SUBLN_EPS)
        o_ref[pl.ds(q0, tq), :] = (ot.T * subln_ref[...] * (1.0 - lam_init)).astype(o_ref.dtype)
        return carry

    lax.fori_loop(0, seq // tq, q_body, 0)


def attn_prompt(q, k, v, lams, subln, batch, seq, lam_init, tq=512):
    t, d = q.shape
    blk = pl.BlockSpec((seq, HEAD_W), lambda b, h: (b, h))
    vec = pl.BlockSpec((1, HEAD_DIM), lambda b, h: (0, 0))
    return pl.pallas_call(
        functools.partial(_attn_prompt_kernel, tq=tq, lam_init=lam_init),
        grid=(batch, N_HEADS),
        in_specs=[blk, blk, blk, vec, vec, vec, vec,
                  pl.BlockSpec((1, HEAD_W), lambda b, h: (0, 0))],
        out_specs=blk,
        out_shape=jax.ShapeDtypeStruct((t, d), BF16),
        scratch_shapes=[pltpu.VMEM((seq // tq, HEAD_W, tq), BF16), pltpu.VMEM((2, HEAD_W, tq), F32)],
        compiler_params=_params(("arbitrary", "arbitrary")),
        name="attn_prompt",
    )(q, k, v, *lams, subln)


def _attn_sample_kernel(pt_ref, q_ref, kn_ref, vn_ref, *rest, n_new, n_pp, lam_init):
    del pt_ref
    kp_refs, vp_refs = rest[:n_pp], rest[n_pp:2 * n_pp]
    bias_ref, biasn_ref, lq1, lk1, lq2, lk2, subln_ref, o_ref, pbuf, m_sc, l_sc, acc_sc = rest[2 * n_pp:]
    p = pl.program_id(1)
    qrows = q_ref[...].astype(BF16)

    @pl.when(p == 0)
    def _():
        m_sc[...] = jnp.full(m_sc.shape, -jnp.inf, F32)
        l_sc[...] = jnp.zeros(l_sc.shape, F32)
        acc_sc[...] = jnp.zeros(acc_sc.shape, F32)

    def attend(k_rows, v_rows, bias):
        r = k_rows.shape[0]
        st = lax.dot_general(k_rows.astype(BF16), qrows, (_NT, ((), ())), preferred_element_type=F32)
        st = st.reshape(r // bias.shape[0], bias.shape[0], LANES) + bias[None]
        m_prev = m_sc[...]
        m_new = jnp.maximum(m_prev, jnp.max(jnp.max(st, axis=0), axis=0, keepdims=True))
        alpha = jnp.exp2(m_prev - m_new)
        pr = jnp.exp2(st - m_new)
        l_sc[...] = alpha * l_sc[...] + jnp.sum(jnp.sum(pr, axis=0), axis=0, keepdims=True)
        m_sc[...] = m_new
        pbuf[0:r, :] = pr.reshape(r, LANES)
        pv = pbuf[pl.ds(0, r // 2, stride=2), :] + pbuf[pl.ds(1, r // 2, stride=2), :]
        vt = v_rows.T.astype(BF16)
        acc_sc[...] = alpha * acc_sc[...] + jnp.dot(vt, pv.astype(BF16), preferred_element_type=F32)

    for kp_ref, vp_ref in zip(kp_refs, vp_refs):
        attend(kp_ref[...], vp_ref[...], bias_ref[...])

    @pl.when(p == pl.num_programs(1) - 1)
    def _():
        attend(kn_ref[...], vn_ref[...], biasn_ref[...])
        lam = _diff_lambda(lq1, lk1, lq2, lk2, lam_init)
        on = acc_sc[...] / l_sc[...]
        d = on - lam * pltpu.roll(on, LANES - n_new, 1)
        d = d * lax.rsqrt(jnp.mean(d * d, axis=0, keepdims=True) + SUBLN_EPS)
        o = d.T * subln_ref[...] * (1.0 - lam_init)
        for h in range(N_HEADS):
            o_ref[:, h * HEAD_W:(h + 1) * HEAD_W] = o[h * 2 * n_new:h * 2 * n_new + n_new]


def attn_sample(q, k_new, v_new, cache_k, cache_v, page_table, lams, subln, layer, lam_init, n_pp=8):
    b, n_new, d = q.shape
    n_pages = page_table.shape[1]
    n_phys = cache_k.shape[1]
    kh = 2 * N_HEADS
    nrow = N_HEADS * 2 * n_new
    n_slot = 16
    assert nrow <= LANES and n_new <= n_slot
    k_view = cache_k.reshape(cache_k.shape[0] * n_phys, PAGE_SIZE * kh, HEAD_DIM)
    v_view = cache_v.reshape(cache_v.shape[0] * n_phys, PAGE_SIZE * N_HEADS, HEAD_W)
    qrows = q.reshape(b, n_new, kh, HEAD_DIM).transpose(0, 2, 1, 3).reshape(b, nrow, HEAD_DIM)
    qrows = jnp.pad(qrows, ((0, 0), (0, LANES - nrow), (0, 0)))
    kn = jnp.pad(k_new, ((0, 0), (0, (n_slot - n_new) * kh), (0, 0)))
    vn = jnp.pad(v_new, ((0, 0), (0, n_slot - n_new), (0, 0))).reshape(b, n_slot * N_HEADS, HEAD_W)
    lane = jnp.arange(LANES)
    own = (jnp.arange(kh)[:, None] == (lane // n_new)[None, :]) | (lane >= nrow)[None, :]
    bias = jnp.where(own, 0.0, -jnp.inf).astype(F32)
    tokn = jnp.arange(n_slot)[:, None, None]
    seen = (tokn < n_new) & (tokn <= (lane % n_new)[None, None, :])
    bias_new = jnp.where((own[None] & seen) | (lane >= nrow)[None, None, :], 0.0, -jnp.inf)
    bias_new = bias_new.astype(F32).reshape(n_slot * kh, LANES)
    tok = pl.BlockSpec((None, n_new, d), lambda i, p, pt: (i, 0, 0))

    def per_seq(rows, width):
        return pl.BlockSpec((None, rows, width), lambda i, p, pt: (i, 0, 0))

    def page_spec(rows, width, k):
        return pl.BlockSpec((None, rows, width),
                            lambda i, p, pt: (layer * n_phys + pt[i * n_pages + p * n_pp + k], 0, 0))

    def const(rows, width):
        return pl.BlockSpec((rows, width), lambda i, p, pt: (0, 0))

    grid_spec = pltpu.PrefetchScalarGridSpec(
        num_scalar_prefetch=1,
        grid=(b, n_pages // n_pp),
        in_specs=[per_seq(LANES, HEAD_DIM), per_seq(n_slot * kh, HEAD_DIM), per_seq(n_slot * N_HEADS, HEAD_W)]
                 + [page_spec(PAGE_SIZE * kh, HEAD_DIM, k) for k in range(n_pp)]
                 + [page_spec(PAGE_SIZE * N_HEADS, HEAD_W, k) for k in range(n_pp)]
                 + [const(kh, LANES), const(n_slot * kh, LANES)]
                 + [const(1, HEAD_DIM)] * 4 + [const(1, HEAD_W)],
        out_specs=tok,
        scratch_shapes=[pltpu.VMEM((PAGE_SIZE * kh, LANES), F32), pltpu.VMEM((1, LANES), F32),
                        pltpu.VMEM((1, LANES), F32), pltpu.VMEM((HEAD_W, LANES), F32)])
    return pl.pallas_call(
        functools.partial(_attn_sample_kernel, n_new=n_new, n_pp=n_pp, lam_init=lam_init),
        grid_spec=grid_spec,
        out_shape=jax.ShapeDtypeStruct((b, n_new, d), F32),
        compiler_params=_params(("arbitrary", "arbitrary")),
        name="attn_sample",
    )(page_table.reshape(-1), qrows, kn, vn, *([k_view] * n_pp), *([v_view] * n_pp), bias, bias_new, *lams, subln)


def _router_kernel(xp_ref, xs_ref, g_ref, wh_ref, wl_ref, b_ref, tri_ref, xn_ref, ids_ref, gates_ref, cnt_ref,
                   run_sc, *, n_p, n_rows):
    i = pl.program_id(0)
    tm = xp_ref.shape[0]

    @pl.when(i == 0)
    def _():
        run_sc[...] = jnp.zeros(run_sc.shape, F32)

    x = jnp.where(i < n_p, xp_ref[...], xs_ref[...])
    xn = _rms(x, g_ref[...], RMS_EPS)
    xn_ref[...] = xn
    xh, xl = _split3(xn)
    d = functools.partial(jnp.dot, preferred_element_type=F32)
    logits = d(xh, wh_ref[...]) + d(xl, wh_ref[...]) + d(xh, wl_ref[...]) + b_ref[...]
    lane = lax.broadcasted_iota(jnp.int32, logits.shape, 1)
    ninf = -jnp.inf
    gl = jnp.where(lane < N_EXPERT_GROUPS, logits, ninf)
    gmax = jnp.max(gl, axis=-1, keepdims=True)
    g_val = 1.0 / jnp.sum(jnp.exp(gl - gmax), axis=-1, keepdims=True)
    g_idx = jnp.min(jnp.where(gl == gmax, lane, LANES), axis=-1, keepdims=True)
    lo = N_EXPERT_GROUPS + EXPERTS_PER_GROUP * g_idx
    el = jnp.where((lane >= lo) & (lane < lo + EXPERTS_PER_GROUP), logits, ninf)
    e1 = jnp.max(el, axis=-1, keepdims=True)
    i1 = jnp.min(jnp.where(el == e1, lane, LANES), axis=-1, keepdims=True)
    el2 = jnp.where(lane == i1, ninf, el)
    e2 = jnp.max(el2, axis=-1, keepdims=True)
    i2 = jnp.min(jnp.where(el2 == e2, lane, LANES), axis=-1, keepdims=True)
    r = jnp.exp(e2 - e1)
    w1 = g_val / (1.0 + r)
    w2 = g_val * r / (1.0 + r)
    chose = jnp.where(lane == i1, 1.0, jnp.where(lane == i2, 1.0, 0.0))
    real = (i * tm + lax.broadcasted_iota(jnp.int32, logits.shape, 0)) < n_rows
    chose = jnp.where(real, chose, 0.0)
    before = jnp.dot(tri_ref[...], chose.astype(BF16), preferred_element_type=F32) + run_sc[...]
    r1 = jnp.sum(jnp.where(lane == i1, before, 0.0), axis=-1, keepdims=True).astype(jnp.int32)
    r2 = jnp.sum(jnp.where(lane == i2, before, 0.0), axis=-1, keepdims=True).astype(jnp.int32)
    run_sc[...] = run_sc[...] + jnp.sum(chose, axis=0, keepdims=True)
    ids_ref[...] = jnp.where(lane == 0, i1 - N_EXPERT_GROUPS, jnp.where(lane == 1, i2 - N_EXPERT_GROUPS,
                             jnp.where(lane == 2, r1, jnp.where(lane == 3, r2, 0))))
    gates_ref[...] = jnp.where(lane == 0, w1, jnp.where(lane == 1, w2, 0.0))

    @pl.when(i == pl.num_programs(0) - 1)
    def _():
        cnt_ref[...] = run_sc[...]


def moe_route(xp, xs, g, w_hi, w_lo, bias, tm):
    (tp, d), ts = xp.shape, xs.shape[0]
    n_p, n_s = tp // tm, -(-ts // tm)
    xs = jnp.pad(xs, ((0, n_s * tm - ts), (0, 0)))
    t = tp + n_s * tm
    row = lambda i: (i, 0)
    const = lambda i: (0, 0)
    tri = jnp.tril(jnp.ones((tm, tm), BF16), -1)
    return pl.pallas_call(
        functools.partial(_router_kernel, n_p=n_p, n_rows=tp + ts),
        grid=(n_p + n_s,),
        in_specs=[pl.BlockSpec((tm, d), lambda i: (jnp.minimum(i, n_p - 1), 0)),
                  pl.BlockSpec((tm, d), lambda i: (jnp.maximum(i - n_p, 0), 0)),
                  pl.BlockSpec((1, d), const),
                  pl.BlockSpec((d, LANES), const), pl.BlockSpec((d, LANES), const),
                  pl.BlockSpec((1, LANES), const), pl.BlockSpec((tm, tm), const)],
        out_specs=[pl.BlockSpec((tm, d), row), pl.BlockSpec((tm, LANES), row), pl.BlockSpec((tm, LANES), row),
                   pl.BlockSpec((1, LANES), const)],
        out_shape=[jax.ShapeDtypeStruct((t, d), F32), jax.ShapeDtypeStruct((t, LANES), jnp.int32),
                   jax.ShapeDtypeStruct((t, LANES), F32), jax.ShapeDtypeStruct((1, LANES), F32)],
        scratch_shapes=[pltpu.VMEM((1, LANES), F32)],
        compiler_params=_params(("arbitrary",)),
        name="moe_route",
    )(xp, xs, g, w_hi, w_lo, bias, tri)


def _row_gather_start(idx_ref, base, n, src_hbm, dst, sem):
    def body(r, c):
        pltpu.make_async_copy(src_hbm.at[pl.ds(idx_ref[base + r], 1), :], dst.at[pl.ds(r, 1), :], sem).start()
        return c
    lax.fori_loop(0, n, body, 0, unroll=8)


def _row_gather_wait(dst, sem):
    pltpu.make_async_copy(dst, dst, sem).wait()


W_CHUNK = 256
W_STAGES = 4


def _expert_kernel(bexp_ref, rtok_ref, nused_ref, nxt_ref, c0_ref, c1_ref, slot_ref,
                   xn_hbm, w13_hbm, w2_hbm, ys_ref, xbuf, sem, wb13, wb2, stage, wsem, *, tm, layer):
    rb = pl.program_id(0)
    nused = nused_ref[0]
    n13 = wb13.shape[1] // W_CHUNK
    nc = n13 + wb2.shape[1] // W_CHUNK

    def start_chunk(e, c):
        s = c % W_STAGES

        @pl.when(c < n13)
        def _():
            pltpu.make_async_copy(w13_hbm.at[layer, e, pl.ds(c * W_CHUNK, W_CHUNK), :], stage.at[s], wsem.at[s]).start()

        @pl.when(c >= n13)
        def _():
            pltpu.make_async_copy(w2_hbm.at[layer, e, pl.ds((c - n13) * W_CHUNK, W_CHUNK), :], stage.at[s],
                                  wsem.at[s]).start()

    def finish_chunk(c, dst):
        s = c % W_STAGES
        pltpu.make_async_copy(stage.at[s], stage.at[s], wsem.at[s]).wait()
        v = stage[s].astype(BF16)

        @pl.when(c < n13)
        def _():
            wb13[dst, pl.ds(pl.multiple_of(c * W_CHUNK, W_CHUNK), W_CHUNK), :] = v

        @pl.when(c >= n13)
        def _():
            wb2[dst, pl.ds(pl.multiple_of((c - n13) * W_CHUNK, W_CHUNK), W_CHUNK), :] = v

    def stream(e, lo, hi, dst):
        @pl.when((lo == 0) & (hi > 0))
        def _():
            for c in range(W_STAGES):
                start_chunk(e, jnp.int32(c))

        def body(c, carry):
            finish_chunk(c, dst)

            @pl.when(c + W_STAGES < nc)
            def _():
                start_chunk(e, c + W_STAGES)
            return carry

        lax.fori_loop(lo, hi, body, 0)

    @pl.when(rb == 0)
    def _():
        stream(bexp_ref[0], 0, nc, 0)
        _row_gather_start(rtok_ref, 0, tm, xn_hbm, xbuf.at[0], sem.at[0])

    @pl.when(rb + 1 < nused)
    def _():
        nxt = (rb + 1) % 2
        _row_gather_start(rtok_ref, (rb + 1) * tm, tm, xn_hbm, xbuf.at[nxt], sem.at[nxt])

    @pl.when(rb < nused)
    def _():
        cur = slot_ref[rb]
        stream(nxt_ref[rb], c0_ref[rb], c1_ref[rb], 1 - cur)
        slot = rb % 2
        _row_gather_wait(xbuf.at[slot], sem.at[slot])
        x = xbuf[slot].astype(BF16)
        h = jnp.dot(x, wb13[cur], preferred_element_type=F32)
        ga, up = h[:, :EXPERT_FF], h[:, EXPERT_FF:]
        a = ga * (1.0 / (1.0 + jnp.exp(-ga))) * up
        ys_ref[...] = jnp.dot(a.astype(BF16), wb2[cur], preferred_element_type=F32)

    @pl.when(rb >= nused)
    def _():
        ys_ref[...] = jnp.zeros(ys_ref.shape, F32)


def moe_experts(xn, w13, w2, layer, block_exp, row_tok, n_used, sched, tm):
    t, d = xn.shape
    n_blocks = block_exp.shape[0]
    anyspace = pl.BlockSpec(memory_space=pl.ANY)
    grid_spec = pltpu.PrefetchScalarGridSpec(
        num_scalar_prefetch=7,
        grid=(n_blocks,),
        in_specs=[anyspace, anyspace, anyspace],
        out_specs=pl.BlockSpec((tm, d), lambda rb, *_: (rb, 0)),
        scratch_shapes=[pltpu.VMEM((2, tm, d), F32), pltpu.SemaphoreType.DMA((2,)),
                        pltpu.VMEM((2, d, 2 * EXPERT_FF), BF16), pltpu.VMEM((2, EXPERT_FF, d), BF16),
                        pltpu.VMEM((W_STAGES, W_CHUNK, d), F32), pltpu.SemaphoreType.DMA((W_STAGES,))])
    return pl.pallas_call(
        functools.partial(_expert_kernel, tm=tm, layer=layer),
        grid_spec=grid_spec,
        out_shape=jax.ShapeDtypeStruct((n_blocks * tm, d), F32),
        compiler_params=_params(("arbitrary",)),
        name="moe_experts",
    )(block_exp, row_tok, n_used, *sched, xn, w13, w2)


def _weight_schedule(padded, block_exp, tm, n_chunks):
    e_idx = jnp.arange(N_EXPERTS, dtype=jnp.int32)
    nonempty = padded > 0
    run = jnp.cumsum(nonempty.astype(jnp.int32)) - 1
    later = jnp.where(nonempty, e_idx, N_EXPERTS)
    nxt = jnp.concatenate([lax.cummin(later[::-1])[::-1][1:], jnp.full((1,), N_EXPERTS, jnp.int32)])
    has_next = nxt < N_EXPERTS
    nblk = padded // tm
    first = (jnp.cumsum(padded) - padded) // tm
    e = block_exp
    j = jnp.arange(block_exp.shape[0], dtype=jnp.int32) - first[e]
    n = jnp.maximum(nblk[e], 1)
    live = has_next[e] & (j >= 0) & (j < nblk[e])
    lo = jnp.where(live, j * n_chunks // n, 0)
    hi = jnp.where(live, (j + 1) * n_chunks // n, 0)
    return (jnp.where(has_next[e], nxt[e], e).astype(jnp.int32), lo.astype(jnp.int32), hi.astype(jnp.int32),
            (run[e] % 2).astype(jnp.int32))


def _combine_kernel(pos_ref, ys_hbm, x_ref, gates_ref, o_ref, ybuf, sem, *, tm):
    i = pl.program_id(0)
    n = pl.num_programs(0)

    def start(tile, slot):
        _row_gather_start(pos_ref, tile * 2 * tm, 2 * tm, ys_hbm, ybuf.at[slot], sem.at[slot])

    @pl.when(i == 0)
    def _():
        start(0, 0)

    @pl.when(i + 1 < n)
    def _():
        start(i + 1, (i + 1) % 2)

    slot = i % 2
    _row_gather_wait(ybuf.at[slot], sem.at[slot])
    g = gates_ref[...]
    o_ref[...] = x_ref[...] + (g[:, 0:1] * ybuf[slot, :tm] + g[:, 1:2] * ybuf[slot, tm:])


def moe_combine(ys, x, gates, pos, tm):
    t, d = x.shape
    pos_tiles = pos.reshape(t // tm, tm, 2).transpose(0, 2, 1).reshape(-1)
    grid_spec = pltpu.PrefetchScalarGridSpec(
        num_scalar_prefetch=1,
        grid=(t // tm,),
        in_specs=[pl.BlockSpec(memory_space=pl.ANY),
                  pl.BlockSpec((tm, d), lambda i, ps: (i, 0)),
                  pl.BlockSpec((tm, LANES), lambda i, ps: (i, 0))],
        out_specs=pl.BlockSpec((tm, d), lambda i, ps: (i, 0)),
        scratch_shapes=[pltpu.VMEM((2, 2 * tm, d), F32), pltpu.SemaphoreType.DMA((2,))])
    return pl.pallas_call(
        functools.partial(_combine_kernel, tm=tm),
        grid_spec=grid_spec,
        out_shape=jax.ShapeDtypeStruct((t, d), F32),
        compiler_params=_params(("arbitrary",)),
        name="moe_combine",
    )(pos_tiles, ys, x, gates)


def hier_moe(xp, xs, g, w_group, b_group, w_expert, b_expert, w13, w2, layer, tm_blk=256, tm_tok=128, tm_route=512):
    tp, d = xp.shape
    pad = LANES - N_EXPERT_GROUPS - N_EXPERTS
    w_cat = jnp.concatenate([w_group, w_expert, jnp.zeros((d, pad), F32)], axis=1)
    b_cat = jnp.concatenate([b_group, b_expert, jnp.zeros((pad,), F32)])[None]
    w_hi = w_cat.astype(BF16)
    w_lo = (w_cat - w_hi.astype(F32)).astype(BF16)
    xn, ids, gates, cnt = moe_route(xp, xs, g, w_hi, w_lo, b_cat, tm_route)
    t = tp + xs.shape[0]
    ids = ids[:t]
    n_assign = 2 * t
    counts = cnt[0, N_EXPERT_GROUPS:N_EXPERT_GROUPS + N_EXPERTS].astype(jnp.int32)
    padded = ((counts + tm_blk - 1) // tm_blk) * tm_blk
    pad_end = jnp.cumsum(padded)
    pad_start = pad_end - padded
    pos = pad_start[ids[:, :2]] + ids[:, 2:4]
    n_blocks = -(-n_assign // tm_blk) + N_EXPERTS
    flat_t = jnp.arange(n_assign, dtype=jnp.int32) // 2
    row_tok = jnp.zeros((n_blocks * tm_blk,), jnp.int32).at[pos.reshape(n_assign)].set(flat_t)
    blk_start = jnp.arange(n_blocks, dtype=jnp.int32) * tm_blk
    block_exp = jnp.minimum(jnp.sum((pad_end[None, :] <= blk_start[:, None]).astype(jnp.int32), axis=1), N_EXPERTS - 1)
    n_used = (pad_end[-1:] // tm_blk).astype(jnp.int32)
    n_wchunks = (w13.shape[2] + w2.shape[2]) // W_CHUNK
    sched = _weight_schedule(padded, block_exp, tm_blk, n_wchunks)
    ys = moe_experts(xn, w13, w2, layer, block_exp, row_tok, n_used, sched, tm_blk)
    return (moe_combine(ys, xp, gates[:tp], pos[:tp], tm_tok),
            moe_combine(ys, xs, gates[tp:t], pos[tp:], min(tm_tok, xs.shape[0])))


def _cmul(ar, ai, br, bi):
    return ar * br - ai * bi, ar * bi + ai * br


def _ssm_prep_kernel(are_ref, aim_ref, ldt_ref, bre_ref, bim_ref, cre_ref, cim_ref,
                     m_ref, bc_ref, cc_ref, a16_ref, a4_ref, *, n_new):
    p = SSM_STATE
    nc = SSM_CHUNK
    a_re, a_im = are_ref[...], aim_ref[...]
    dt = jnp.exp(ldt_ref[...])
    mag = jnp.exp(dt * a_re)
    ab_re = mag * jnp.cos(dt * a_im)
    ab_im = mag * jnp.sin(dt * a_im)
    den = a_re * a_re + a_im * a_im
    nr = ab_re - 1.0
    fr = (nr * a_re + ab_im * a_im) / den
    fi = (ab_im * a_re - nr * a_im) / den
    bt_re, bt_im = bre_ref[...], bim_ref[...]
    bb_re = fr * bt_re - fi * bt_im
    bb_im = fr * bt_im + fi * bt_re
    c_re, c_im = cre_ref[...], cim_ref[...]
    pw = [(jnp.ones_like(ab_re), jnp.zeros_like(ab_re))]
    for _ in range(nc):
        pw.append(_cmul(pw[-1][0], pw[-1][1], ab_re, ab_im))
    wst = []
    for tau in range(nc + 1):
        wr, wi = _cmul(c_re, c_im, pw[tau][0], pw[tau][1])
        wst.append(jnp.concatenate([wr, -wi], axis=1))
    wstack = jnp.concatenate(wst[:nc], axis=0)
    bs_t = jnp.concatenate([bb_re, bb_im], axis=1)
    q = _dot3(bs_t, wstack, _NT)
    qpad = jnp.concatenate([jnp.zeros_like(q), q], axis=1)
    w = nc * SSM_GROUP_CH
    rows = []
    for s in range(nc):
        off = (nc - s) * SSM_GROUP_CH
        rows.append(qpad[:, off:off + w])
    m_ref[...] = jnp.concatenate(rows, axis=0).astype(m_ref.dtype)
    bc = []
    for s in range(nc):
        xr, xi = _cmul(bb_re, bb_im, pw[nc - 1 - s][0], pw[nc - 1 - s][1])
        bc.append(jnp.concatenate([xr, xi], axis=1))
    bc_ref[...] = jnp.concatenate(bc, axis=0).astype(bc_ref.dtype)
    cc_ref[...] = jnp.concatenate(wst[1:], axis=0).astype(cc_ref.dtype)
    a16_ref[...] = jnp.concatenate([pw[nc][0], pw[nc][1]], axis=1)
    a4_ref[...] = jnp.concatenate([pw[n_new][0], pw[n_new][1]], axis=1)


def ssm_prep(a_re, a_im, log_dt, b_re, b_im, c_re, c_im, n_new):
    g, p = a_re.shape
    gc = SSM_GROUP_CH
    w = SSM_CHUNK * gc
    vec = pl.BlockSpec((None, 1, p), lambda i: (i, 0, 0))
    mat = pl.BlockSpec((None, gc, p), lambda i: (i, 0, 0))
    bt_re = b_re.transpose(0, 2, 1)
    bt_im = b_im.transpose(0, 2, 1)
    return pl.pallas_call(
        functools.partial(_ssm_prep_kernel, n_new=n_new),
        grid=(g,),
        in_specs=[vec, vec, pl.BlockSpec((None, 1, 1), lambda i: (i, 0, 0)), mat, mat, mat, mat],
        out_specs=[pl.BlockSpec((None, w, w), lambda i: (i, 0, 0)),
                   pl.BlockSpec((None, w, 2 * p), lambda i: (i, 0, 0)),
                   pl.BlockSpec((None, w, 2 * p), lambda i: (i, 0, 0)),
                   pl.BlockSpec((None, 1, 2 * p), lambda i: (i, 0, 0)),
                   pl.BlockSpec((None, 1, 2 * p), lambda i: (i, 0, 0))],
        out_shape=[jax.ShapeDtypeStruct((g, w, w), BF16),
                   jax.ShapeDtypeStruct((g, w, 2 * p), BF16),
                   jax.ShapeDtypeStruct((g, w, 2 * p), BF16),
                   jax.ShapeDtypeStruct((g, 1, 2 * p), F32),
                   jax.ShapeDtypeStruct((g, 1, 2 * p), F32)],
        compiler_params=_params(("arbitrary",)),
        name="ssm_prep",
    )(a_re[:, None], a_im[:, None], log_dt[:, None, None], bt_re, bt_im, c_re, c_im)


GROUPS_PER_VREG = LANES // SSM_GROUP_CH


def _chunk_perm():
    n = GROUPS_PER_VREG
    idx = np.arange(n * LANES)
    a, b, c = idx // LANES, (idx // SSM_GROUP_CH) % n, idx % SSM_GROUP_CH
    m = np.zeros((n * LANES, n * LANES), np.float32)
    m[idx, (b * n + a) * SSM_GROUP_CH + c] = 1.0
    return jnp.asarray(m, BF16)


def _ssm_in_kernel(x_ref, g_ref, p_ref, u_ref, scr):
    tm, d = x_ref.shape
    nc, nj, n = tm // SSM_CHUNK, d // LANES, GROUPS_PER_VREG
    xn = _rms(x_ref[...], g_ref[...], RMS_EPS)
    for j in range(nj):
        scr[pl.ds(j, tm, stride=nj), :] = xn[:, j * LANES:(j + 1) * LANES]
    perm = p_ref[...]
    for j in range(nj):
        for h in range(SSM_CHUNK // n):
            lhs = jnp.concatenate([scr[pl.ds((h * n + sl) * nj + j, nc, stride=SSM_CHUNK * nj), :]
                                   for sl in range(n)], axis=1).astype(BF16)
            r = jnp.dot(lhs, perm, preferred_element_type=F32)
            for gl in range(n):
                u_ref[j * n + gl, :, h * LANES:(h + 1) * LANES] = r[:, gl * LANES:(gl + 1) * LANES].astype(BF16)


def ssm_chunk_in(x, g, perm, tm):
    t, d = x.shape
    ng, w = d // SSM_GROUP_CH, SSM_CHUNK * SSM_GROUP_CH
    nc = tm // SSM_CHUNK
    return pl.pallas_call(
        _ssm_in_kernel,
        grid=(t // tm,),
        in_specs=[pl.BlockSpec((tm, d), lambda i: (i, 0)), pl.BlockSpec((1, d), lambda i: (0, 0)),
                  pl.BlockSpec(perm.shape, lambda i: (0, 0))],
        out_specs=pl.BlockSpec((ng, nc, w), lambda i: (0, i, 0)),
        out_shape=jax.ShapeDtypeStruct((ng, t // SSM_CHUNK, w), BF16),
        scratch_shapes=[pltpu.VMEM((tm * (d // LANES), LANES), F32)],
        compiler_params=_params(("arbitrary",)),
        name="ssm_chunk_in",
    )(x, g, perm)


def _ssm_out_kernel(z_ref, p_ref, o_ref, scr):
    tm, d = o_ref.shape
    nc, nj, n = tm // SSM_CHUNK, d // LANES, GROUPS_PER_VREG
    perm = p_ref[...]
    for j in range(nj):
        for h in range(SSM_CHUNK // n):
            lhs = jnp.concatenate([z_ref[j * n + gl, :, h * LANES:(h + 1) * LANES] for gl in range(n)], axis=1)
            r = jnp.dot(lhs, perm, preferred_element_type=F32)
            for tl in range(n):
                scr[pl.ds((h * n + tl) * nj + j, nc, stride=SSM_CHUNK * nj), :] = r[:, tl * LANES:(tl + 1) * LANES]
    for j in range(nj):
        o_ref[:, j * LANES:(j + 1) * LANES] = scr[pl.ds(j, tm, stride=nj), :].astype(o_ref.dtype)


def ssm_chunk_out(z, perm, tm, d):
    ng, n_c, w = z.shape
    nc = tm // SSM_CHUNK
    return pl.pallas_call(
        _ssm_out_kernel,
        grid=(n_c // nc,),
        in_specs=[pl.BlockSpec((ng, nc, w), lambda i: (0, i, 0)), pl.BlockSpec(perm.shape, lambda i: (0, 0))],
        out_specs=pl.BlockSpec((tm, d), lambda i: (i, 0)),
        out_shape=jax.ShapeDtypeStruct((n_c * SSM_CHUNK, d), z.dtype),
        scratch_shapes=[pltpu.VMEM((tm * (d // LANES), LANES), F32)],
        compiler_params=_params(("arbitrary",)),
        name="ssm_chunk_out",
    )(z, perm)


def _ssm_x_kernel(u_ref, bc_ref, x_ref, *, k):
    for gi in range(u_ref.shape[0]):
        bc = bc_ref[gi]
        x_ref[gi] = jnp.dot(u_ref[gi, :, :k], bc[bc.shape[0] - k:], preferred_element_type=F32)


def _ssm_scan_kernel(xp_ref, xs_ref, h0_ref, a16_ref, a4_ref, hs_ref, hfin_ref, hsfin_ref, *, n_chunks, batch):
    p = SSM_STATE

    def coef(a):
        ar, ai = a[:, :, :p], a[:, :, p:]
        return jnp.concatenate([ar, ar], axis=-1), jnp.concatenate([-ai, ai], axis=-1)

    def step(h, c1, c2):
        return c1 * h + c2 * pltpu.roll(h, p, 2)

    c1, c2 = coef(a16_ref[...])

    def body(c, h):
        rows = pl.ds(c, batch, stride=n_chunks)
        hs_ref[:, rows, :] = h
        return step(h, c1, c2) + xp_ref[:, rows, :]

    h0 = jnp.zeros((xp_ref.shape[0], batch, 2 * p), F32)
    hfin_ref[...] = lax.fori_loop(0, n_chunks, body, h0)
    d1, d2 = coef(a4_ref[...])
    hsfin_ref[...] = step(h0_ref[...], d1, d2) + xs_ref[...]


def _ssm_y_kernel(u_ref, hs_ref, m_ref, cc_ref, d_ref, z_ref):
    c = math.sqrt(2.0 / math.pi)
    for gi in range(u_ref.shape[0]):
        u = u_ref[gi]
        y = jnp.dot(u, m_ref[gi], preferred_element_type=F32)
        y = y + lax.dot_general(hs_ref[gi].astype(BF16), cc_ref[gi], (_NT, ((), ())), preferred_element_type=F32)
        y = y + d_ref[gi] * u.astype(F32)
        z = 0.5 * y * (1.0 + jnp.tanh(c * (y + 0.044715 * (y * y * y))))
        z_ref[gi] = z.astype(z_ref.dtype)


def ssm_mix(up, us, h0_g, ops, d_tile, n_chunks, batch, n_new, g_blk=8, g_scan=16):
    m_g, bc_g, cc_g, a16, a4 = ops
    g, _, w = up.shape
    p2 = 2 * SSM_STATE
    n_s = us.shape[1]

    def blk(gb, *shape):
        return pl.BlockSpec((gb,) + shape, lambda i: (i,) + (0,) * len(shape))

    def chunk_state(u, k):
        r = u.shape[1]
        return pl.pallas_call(
            functools.partial(_ssm_x_kernel, k=k),
            grid=(g // g_blk,),
            in_specs=[blk(g_blk, r, w), blk(g_blk, w, p2)],
            out_specs=blk(g_blk, r, p2),
            out_shape=jax.ShapeDtypeStruct((g, r, p2), F32),
            compiler_params=_params(("arbitrary",)),
            name="ssm_chunk_state",
        )(u, bc_g)

    def output(u, hs):
        r = u.shape[1]
        return pl.pallas_call(
            _ssm_y_kernel,
            grid=(g // g_blk,),
            in_specs=[blk(g_blk, r, w), blk(g_blk, r, p2), blk(g_blk, w, w), blk(g_blk, w, p2), blk(g_blk, 1, w)],
            out_specs=blk(g_blk, r, w),
            out_shape=jax.ShapeDtypeStruct((g, r, w), BF16),
            compiler_params=_params(("arbitrary",)),
            name="ssm_output",
        )(u, hs, m_g, cc_g, d_tile)

    xp = chunk_state(up, w)
    xs = chunk_state(us, n_new * SSM_GROUP_CH)
    r = up.shape[1]
    hs, hfin, hsfin = pl.pallas_call(
        functools.partial(_ssm_scan_kernel, n_chunks=n_chunks, batch=batch),
        grid=(g // g_scan,),
        in_specs=[blk(g_scan, r, p2), blk(g_scan, n_s, p2), blk(g_scan, n_s, p2), blk(g_scan, 1, p2),
                  blk(g_scan, 1, p2)],
        out_specs=[blk(g_scan, r, p2), blk(g_scan, batch, p2), blk(g_scan, n_s, p2)],
        out_shape=[jax.ShapeDtypeStruct((g, r, p2), F32),
                   jax.ShapeDtypeStruct((g, batch, p2), F32),
                   jax.ShapeDtypeStruct((g, n_s, p2), F32)],
        compiler_params=_params(("arbitrary",)),
        name="ssm_scan",
    )(xp, xs, h0_g, a16, a4)
    return output(up, hs), output(us, h0_g), hfin, hsfin


def kernel(x_prompt, x_sample, cache_k, cache_v, state_ssm_re, state_ssm_im, page_table, norm_mix, norm_ffn, norm_final, attn_w_qkv, attn_lambda_q1, attn_lambda_k1, attn_lambda_q2, attn_lambda_k2, attn_subln, attn_w_o, ssm_A_re, ssm_A_im, ssm_log_dt, ssm_B_re, ssm_B_im, ssm_C_re, ssm_C_im, ssm_D, ssm_glu_w, ssm_glu_b, moe_w_group, moe_b_group, moe_w_expert, moe_b_expert, moe_w13, moe_w2):
    batch, seq, d = x_prompt.shape
    dec_b, dec_s, _ = x_sample.shape
    depth = norm_mix.shape[0]
    tp, ts = batch * seq, dec_b * dec_s
    xp = x_prompt.reshape(tp, d)
    xs = x_sample.reshape(ts, d)
    g, pst, gc = ssm_B_re.shape[1:]
    n_chunks = seq // SSM_CHUNK
    k_p, v_p, k_s, v_s = [], [], [], []
    hr_p, hi_p, hr_s, hi_s = [], [], [], []
    for i in range(depth):
        gm = norm_mix[i][None]
        if i % 2 == 0:
            a = i // 2
            lam_init = 0.8 - 0.6 * math.exp(-0.3 * i)
            lams = (attn_lambda_q1[a][None], attn_lambda_k1[a][None], attn_lambda_q2[a][None], attn_lambda_k2[a][None])
            subln = attn_subln[a][None]
            w_qkv = attn_w_qkv[a].astype(BF16)
            w_o = attn_w_o[a].astype(BF16)
            qp, khp, kp, vp = qkv_proj(xp, gm, w_qkv, 1024, 512)
            qs, _, ks, vs = qkv_proj(xs, gm, w_qkv, ts, 512)
            op = attn_prompt(qp, khp, vp, lams, subln, batch, seq, lam_init)
            os_ = attn_sample(qs.reshape(dec_b, dec_s, d), ks.reshape(dec_b, dec_s * 2 * N_HEADS, HEAD_DIM),
                              vs.reshape(dec_b, dec_s, d), cache_k, cache_v, page_table, lams, subln, a, lam_init)
            xp = matmul_residual(op, w_o, xp, 1024, 512)
            xs = matmul_residual(os_.reshape(ts, d), w_o, xs, ts, 512)
            k_p.append(kp.reshape(batch, seq, N_HEADS, 2, HEAD_DIM))
            v_p.append(vp.reshape(batch, seq, N_HEADS, HEAD_W))
            k_s.append(ks.reshape(dec_b, dec_s, N_HEADS, 2, HEAD_DIM))
            v_s.append(vs.reshape(dec_b, dec_s, N_HEADS, HEAD_W))
        else:
            s = i // 2
            ops = ssm_prep(ssm_A_re[s], ssm_A_im[s], ssm_log_dt[s], ssm_B_re[s], ssm_B_im[s],
                           ssm_C_re[s], ssm_C_im[s], dec_s)
            perm = _chunk_perm()
            up = ssm_chunk_in(xp, gm, perm, 1024)
            hs = rms_norm(xs, gm, ts)
            us = hs.reshape(dec_b, dec_s, g, gc).transpose(2, 0, 1, 3).reshape(g, dec_b, dec_s * gc)
            us = jnp.pad(us, ((0, 0), (0, 0), (0, (SSM_CHUNK - dec_s) * gc))).astype(BF16)
            h0_g = jnp.concatenate([state_ssm_re[s], state_ssm_im[s]], axis=-1).transpose(1, 0, 2)
            d_tile = jnp.tile(ssm_D[s], (1, SSM_CHUNK))[:, None]
            zp_g, zs_g, hfin, hsfin = ssm_mix(up, us, h0_g, ops, d_tile, n_chunks, batch, dec_s)
            zp = ssm_chunk_out(zp_g, perm, 1024, d)
            zs = zs_g[:, :, :dec_s * gc].reshape(g, dec_b, dec_s, gc).transpose(1, 2, 0, 3).reshape(ts, d)
            glu_w = ssm_glu_w[s].astype(BF16)
            glu_b = ssm_glu_b[s][None]
            xp = glu_residual(zp, glu_w, glu_b, xp, 1024, 512)
            xs = glu_residual(zs, glu_w, glu_b, xs, ts, 512)
            hfin = hfin.transpose(1, 0, 2)
            hsfin = hsfin.transpose(1, 0, 2)
            hr_p.append(hfin[..., :pst])
            hi_p.append(hfin[..., pst:])
            hr_s.append(hsfin[..., :pst])
            hi_s.append(hsfin[..., pst:])
        xp, xs = hier_moe(xp, xs, norm_ffn[i][None], moe_w_group[i], moe_b_group[i], moe_w_expert[i],
                          moe_b_expert[i], moe_w13, moe_w2, i)
    y_prompt = rms_norm(xp, norm_final[None], 1024).reshape(batch, seq, d)
    y_sample = rms_norm(xs, norm_final[None], ts).reshape(dec_b, dec_s, d)
    return (y_prompt, y_sample, jnp.stack(k_p), jnp.stack(v_p), jnp.stack(k_s), jnp.stack(v_s),
            jnp.stack(hr_p), jnp.stack(hi_p), jnp.stack(hr_s), jnp.stack(hi_s))
```

```python
import functools
import math

import numpy as np
import jax
import jax.numpy as jnp
from jax import lax
from jax.experimental import pallas as pl
from jax.experimental.pallas import tpu as pltpu

F32 = jnp.float32
BF16 = jnp.bfloat16

RMS_EPS = 1e-6
SUBLN_EPS = 1e-5
HEAD_DIM = 128
HEAD_W = 2 * HEAD_DIM
N_HEADS = 8
Q_PRESCALE = HEAD_DIM ** -0.5 * math.log2(math.e)
PAGE_SIZE = 128
N_EXPERT_GROUPS = 4
EXPERTS_PER_GROUP = 8
N_EXPERTS = N_EXPERT_GROUPS * EXPERTS_PER_GROUP
EXPERT_FF = 1024
SSM_GROUP_CH = 16
SSM_STATE = 64
SSM_CHUNK = 16
LANES = 128
VMEM_LIMIT = 56 * 1024 * 1024


def _params(sem, vmem=VMEM_LIMIT):
    return pltpu.CompilerParams(dimension_semantics=sem, vmem_limit_bytes=vmem)


def _rms(x, g, eps):
    return x * lax.rsqrt(jnp.mean(x * x, axis=-1, keepdims=True) + eps) * g


def _split3(x):
    hi = x.astype(BF16)
    lo = (x - hi.astype(F32)).astype(BF16)
    return hi, lo


def _dot3(a, b, dims):
    ah, al = _split3(a)
    bh, bl = _split3(b)
    dn = (dims, ((), ()))
    d = functools.partial(lax.dot_general, dimension_numbers=dn, preferred_element_type=F32)
    return d(ah, bh) + d(al, bh) + d(ah, bl)


_NT = ((1,), (1,))
_NN = ((1,), (0,))


def _qkv_kernel(x_ref, g_ref, w_ref, q_ref, kh_ref, kn_ref, v_ref, xn_ref, *, nj):
    j = pl.program_id(1)
    tn = w_ref.shape[1]

    @pl.when(j == 0)
    def _():
        xn_ref[...] = _rms(x_ref[...], g_ref[...], RMS_EPS).astype(BF16)

    acc = jnp.dot(xn_ref[...], w_ref[...], preferred_element_type=F32)

    @pl.when(j < nj)
    def _():
        q_ref[...] = (acc * Q_PRESCALE).astype(q_ref.dtype)

    for jj in range(nj):
        @pl.when(j == nj + jj)
        def _(jj=jj):
            kh_ref[...] = acc.astype(kh_ref.dtype)
            for i in range(tn // HEAD_DIM):
                kn_ref[:, jj * (tn // HEAD_DIM) + i, :] = acc[:, i * HEAD_DIM:(i + 1) * HEAD_DIM]

    @pl.when(j >= 2 * nj)
    def _():
        v_ref[...] = acc


def qkv_proj(x, g, w_bf, tm, tn):
    t, d = x.shape
    nj = d // tn

    def omap(part):
        return lambda i, j: (i, jnp.clip(j - part * nj, 0, nj - 1))

    return pl.pallas_call(
        functools.partial(_qkv_kernel, nj=nj),
        grid=(t // tm, 3 * nj),
        in_specs=[pl.BlockSpec((tm, d), lambda i, j: (i, 0)),
                  pl.BlockSpec((1, d), lambda i, j: (0, 0)),
                  pl.BlockSpec((d, tn), lambda i, j: (0, j))],
        out_specs=[pl.BlockSpec((tm, tn), omap(0)), pl.BlockSpec((tm, tn), omap(1)),
                   pl.BlockSpec((tm, d // HEAD_DIM, HEAD_DIM), lambda i, j: (i, 0, 0)),
                   pl.BlockSpec((tm, tn), omap(2))],
        out_shape=[jax.ShapeDtypeStruct((t, d), BF16), jax.ShapeDtypeStruct((t, d), BF16),
                   jax.ShapeDtypeStruct((t, d // HEAD_DIM, HEAD_DIM), F32), jax.ShapeDtypeStruct((t, d), F32)],
        scratch_shapes=[pltpu.VMEM((tm, d), BF16)],
        compiler_params=_params(("arbitrary", "arbitrary")),
        name="qkv_proj",
    )(x, g, w_bf)


def _mm_res_kernel(a_ref, w_ref, x_ref, o_ref):
    o_ref[...] = x_ref[...] + jnp.dot(a_ref[...].astype(BF16), w_ref[...], preferred_element_type=F32)


def matmul_residual(a, w_bf, x, tm, tn):
    t, kd = a.shape
    n = w_bf.shape[1]
    return pl.pallas_call(
        _mm_res_kernel,
        grid=(t // tm, n // tn),
        in_specs=[pl.BlockSpec((tm, kd), lambda i, j: (i, 0)),
                  pl.BlockSpec((kd, tn), lambda i, j: (0, j)),
                  pl.BlockSpec((tm, tn), lambda i, j: (i, j))],
        out_specs=pl.BlockSpec((tm, tn), lambda i, j: (i, j)),
        out_shape=jax.ShapeDtypeStruct((t, n), F32),
        compiler_params=_params(("arbitrary", "arbitrary")),
        name="matmul_residual",
    )(a, w_bf, x)


def _glu_kernel(z_ref, w1_ref, w2_ref, b1_ref, b2_ref, x_ref, o_ref):
    z = z_ref[...]
    a = jnp.dot(z, w1_ref[...], preferred_element_type=F32) + b1_ref[...]
    b = jnp.dot(z, w2_ref[...], preferred_element_type=F32) + b2_ref[...]
    o_ref[...] = x_ref[...] + a * (1.0 / (1.0 + jnp.exp(-b)))


def glu_residual(z, w_bf, bias, x, tm, tn):
    t, d = z.shape
    nj = d // tn
    return pl.pallas_call(
        _glu_kernel,
        grid=(t // tm, nj),
        in_specs=[pl.BlockSpec((tm, d), lambda i, j: (i, 0)),
                  pl.BlockSpec((d, tn), lambda i, j: (0, j)),
                  pl.BlockSpec((d, tn), lambda i, j: (0, j + nj)),
                  pl.BlockSpec((1, tn), lambda i, j: (0, j)),
                  pl.BlockSpec((1, tn), lambda i, j: (0, j + nj)),
                  pl.BlockSpec((tm, tn), lambda i, j: (i, j))],
        out_specs=pl.BlockSpec((tm, tn), lambda i, j: (i, j)),
        out_shape=jax.ShapeDtypeStruct((t, d), F32),
        compiler_params=_params(("arbitrary", "arbitrary")),
        name="glu_residual",
    )(z, w_bf, w_bf, bias, bias, x)


def _norm_kernel(x_ref, g_ref, o_ref):
    o_ref[...] = _rms(x_ref[...], g_ref[...], RMS_EPS).astype(o_ref.dtype)


def rms_norm(x, g, tm):
    t, d = x.shape
    return pl.pallas_call(
        _norm_kernel,
        grid=(t // tm,),
        in_specs=[pl.BlockSpec((tm, d), lambda i: (i, 0)),
                  pl.BlockSpec((1, d), lambda i: (0, 0))],
        out_specs=pl.BlockSpec((tm, d), lambda i: (i, 0)),
        out_shape=jax.ShapeDtypeStruct((t, d), F32),
        compiler_params=_params(("arbitrary",)),
        name="rms_norm",
    )(x, g)


def _diff_lambda(lq1, lk1, lq2, lk2, lam_init):
    return (jnp.exp(jnp.sum(lq1[...] * lk1[...], axis=-1, keepdims=True))
            - jnp.exp(jnp.sum(lq2[...] * lk2[...], axis=-1, keepdims=True)) + lam_init)


def _attn_prompt_kernel(q_ref, k_ref, v_ref, lq1, lk1, lq2, lk2, subln_ref, o_ref,
                        vtb, acc_sc, *, tq, lam_init):
    seq = q_ref.shape[0]
    for j in range(seq // tq):
        vtb[j] = v_ref[j * tq:(j + 1) * tq, :].T.astype(BF16)
    lam = _diff_lambda(lq1, lk1, lq2, lk2, lam_init)
    krow = lax.broadcasted_iota(jnp.int32, (tq, tq), 0)
    qcol = lax.broadcasted_iota(jnp.int32, (tq, tq), 1)
    causal = krow <= qcol

    def q_body(qi, carry):
        q0 = pl.multiple_of(qi * tq, tq)
        qt = q_ref[pl.ds(q0, tq), :]
        acc_sc[...] = jnp.zeros(acc_sc.shape, F32)

        def kv_step(kj, masked, st):
            k0 = pl.multiple_of(kj * tq, tq)
            kt = k_ref[pl.ds(k0, tq), :]
            vt = vtb[kj]
            out = []
            for s in range(2):
                ls = slice(s * HEAD_DIM, (s + 1) * HEAD_DIM)
                sc = lax.dot_general(kt[:, ls], qt[:, ls], (_NT, ((), ())), preferred_element_type=F32)
                if masked:
                    sc = jnp.where(causal, sc, -jnp.inf)
                m_prev, l_prev = st[2 * s], st[2 * s + 1]
                m_new = jnp.maximum(m_prev, jnp.max(sc, axis=0, keepdims=True))
                alpha = jnp.exp2(m_prev - m_new)
                p = jnp.exp2(sc - m_new)
                out += [m_new, alpha * l_prev + jnp.sum(p, axis=0, keepdims=True)]
                acc_sc[s] = alpha * acc_sc[s] + jnp.dot(vt, p.astype(BF16), preferred_element_type=F32)
            return tuple(out)

        ninf = jnp.full((1, tq), -jnp.inf, F32)
        zero = jnp.zeros((1, tq), F32)
        st = lax.fori_loop(0, qi, lambda kj, st: kv_step(kj, False, st), (ninf, zero, ninf, zero))
        _, l0, _, l1 = kv_step(qi, True, st)
        ot = acc_sc[0] / l0 - lam * (acc_sc[1] / l1)
        ot = ot * lax.rsqrt(jnp.mean(ot * ot, axis=0, keepdims=True) + SUBLN_EPS)
        o_ref[pl.ds(q0, tq), :] = (ot.T * subln_ref[...] * (1.0 - lam_init)).astype(o_ref.dtype)
        return carry

    lax.fori_loop(0, seq // tq, q_body, 0)


def attn_prompt(q, k, v, lams, subln, batch, seq, lam_init, tq=512):
    t, d = q.shape
    blk = pl.BlockSpec((seq, HEAD_W), lambda b, h: (b, h))
    vec = pl.BlockSpec((1, HEAD_DIM), lambda b, h: (0, 0))
    return pl.pallas_call(
        functools.partial(_attn_prompt_kernel, tq=tq, lam_init=lam_init),
        grid=(batch, N_HEADS),
        in_specs=[blk, blk, blk, vec, vec, vec, vec,
                  pl.BlockSpec((1, HEAD_W), lambda b, h: (0, 0))],
        out_specs=blk,
        out_shape=jax.ShapeDtypeStruct((t, d), BF16),
        scratch_shapes=[pltpu.VMEM((seq // tq, HEAD_W, tq), BF16), pltpu.VMEM((2, HEAD_W, tq), F32)],
        compiler_params=_params(("arbitrary", "arbitrary")),
        name="attn_prompt",
    )(q, k, v, *lams, subln)


def _attn_sample_kernel(pt_ref, q_ref, kn_ref, vn_ref, *rest, n_new, n_pp, lam_init):
    del pt_ref
    kp_refs, vp_refs = rest[:n_pp], rest[n_pp:2 * n_pp]
    bias_ref, biasn_ref, lq1, lk1, lq2, lk2, subln_ref, o_ref, pbuf, m_sc, l_sc, acc_sc = rest[2 * n_pp:]
    p = pl.program_id(1)
    qrows = q_ref[...].astype(BF16)

    @pl.when(p == 0)
    def _():
        m_sc[...] = jnp.full(m_sc.shape, -jnp.inf, F32)
        l_sc[...] = jnp.zeros(l_sc.shape, F32)
        acc_sc[...] = jnp.zeros(acc_sc.shape, F32)

    def attend(k_rows, v_rows, bias):
        r = k_rows.shape[0]
        st = lax.dot_general(k_rows.astype(BF16), qrows, (_NT, ((), ())), preferred_element_type=F32)
        st = st.reshape(r // bias.shape[0], bias.shape[0], LANES) + bias[None]
        m_prev = m_sc[...]
        m_new = jnp.maximum(m_prev, jnp.max(jnp.max(st, axis=0), axis=0, keepdims=True))
        alpha = jnp.exp2(m_prev - m_new)
        pr = jnp.exp2(st - m_new)
        l_sc[...] = alpha * l_sc[...] + jnp.sum(jnp.sum(pr, axis=0), axis=0, keepdims=True)
        m_sc[...] = m_new
        pbuf[0:r, :] = pr.reshape(r, LANES)
        pv = pbuf[pl.ds(0, r // 2, stride=2), :] + pbuf[pl.ds(1, r // 2, stride=2), :]
        vt = v_rows.T.astype(BF16)
        acc_sc[...] = alpha * acc_sc[...] + jnp.dot(vt, pv.astype(BF16), preferred_element_type=F32)

    for kp_ref, vp_ref in zip(kp_refs, vp_refs):
        attend(kp_ref[...], vp_ref[...], bias_ref[...])

    @pl.when(p == pl.num_programs(1) - 1)
    def _():
        attend(kn_ref[...], vn_ref[...], biasn_ref[...])
        lam = _diff_lambda(lq1, lk1, lq2, lk2, lam_init)
        on = acc_sc[...] / l_sc[...]
        d = on - lam * pltpu.roll(on, LANES - n_new, 1)
        d = d * lax.rsqrt(jnp.mean(d * d, axis=0, keepdims=True) + SUBLN_EPS)
        o = d.T * subln_ref[...] * (1.0 - lam_init)
        for h in range(N_HEADS):
            o_ref[:, h * HEAD_W:(h + 1) * HEAD_W] = o[h * 2 * n_new:h * 2 * n_new + n_new]


def attn_sample(q, k_new, v_new, cache_k, cache_v, page_table, lams, subln, layer, lam_init, n_pp=8):
    b, n_new, d = q.shape
    n_pages = page_table.shape[1]
    n_phys = cache_k.shape[1]
    kh = 2 * N_HEADS
    nrow = N_HEADS * 2 * n_new
    n_slot = 16
    assert nrow <= LANES and n_new <= n_slot
    k_view = cache_k.reshape(cache_k.shape[0] * n_phys, PAGE_SIZE * kh, HEAD_DIM)
    v_view = cache_v.reshape(cache_v.shape[0] * n_phys, PAGE_SIZE * N_HEADS, HEAD_W)
    qrows = q.reshape(b, n_new, kh, HEAD_DIM).transpose(0, 2, 1, 3).reshape(b, nrow, HEAD_DIM)
    qrows = jnp.pad(qrows, ((0, 0), (0, LANES - nrow), (0, 0)))
    kn = jnp.pad(k_new, ((0, 0), (0, (n_slot - n_new) * kh), (0, 0)))
    vn = jnp.pad(v_new, ((0, 0), (0, n_slot - n_new), (0, 0))).reshape(b, n_slot * N_HEADS, HEAD_W)
    lane = jnp.arange(LANES)
    own = (jnp.arange(kh)[:, None] == (lane // n_new)[None, :]) | (lane >= nrow)[None, :]
    bias = jnp.where(own, 0.0, -jnp.inf).astype(F32)
    tokn = jnp.arange(n_slot)[:, None, None]
    seen = (tokn < n_new) & (tokn <= (lane % n_new)[None, None, :])
    bias_new = jnp.where((own[None] & seen) | (lane >= nrow)[None, None, :], 0.0, -jnp.inf)
    bias_new = bias_new.astype(F32).reshape(n_slot * kh, LANES)
    tok = pl.BlockSpec((None, n_new, d), lambda i, p, pt: (i, 0, 0))

    def per_seq(rows, width):
        return pl.BlockSpec((None, rows, width), lambda i, p, pt: (i, 0, 0))

    def page_spec(rows, width, k):
        return pl.BlockSpec((None, rows, width),
                            lambda i, p, pt: (layer * n_phys + pt[i * n_pages + p * n_pp + k], 0, 0))

    def const(rows, width):
        return pl.BlockSpec((rows, width), lambda i, p, pt: (0, 0))

    grid_spec = pltpu.PrefetchScalarGridSpec(
        num_scalar_prefetch=1,
        grid=(b, n_pages // n_pp),
        in_specs=[per_seq(LANES, HEAD_DIM), per_seq(n_slot * kh, HEAD_DIM), per_seq(n_slot * N_HEADS, HEAD_W)]
                 + [page_spec(PAGE_SIZE * kh, HEAD_DIM, k) for k in range(n_pp)]
                 + [page_spec(PAGE_SIZE * N_HEADS, HEAD_W, k) for k in range(n_pp)]
                 + [const(kh, LANES), const(n_slot * kh, LANES)]
                 + [const(1, HEAD_DIM)] * 4 + [const(1, HEAD_W)],
        out_specs=tok,
        scratch_shapes=[pltpu.VMEM((PAGE_SIZE * kh, LANES), F32), pltpu.VMEM((1, LANES), F32),
                        pltpu.VMEM((1, LANES), F32), pltpu.VMEM((HEAD_W, LANES), F32)])
    return pl.pallas_call(
        functools.partial(_attn_sample_kernel, n_new=n_new, n_pp=n_pp, lam_init=lam_init),
        grid_spec=grid_spec,
        out_shape=jax.ShapeDtypeStruct((b, n_new, d), F32),
        compiler_params=_params(("arbitrary", "arbitrary")),
        name="attn_sample",
    )(page_table.reshape(-1), qrows, kn, vn, *([k_view] * n_pp), *([v_view] * n_pp), bias, bias_new, *lams, subln)


def _router_kernel(xp_ref, xs_ref, g_ref, wh_ref, wl_ref, b_ref, tri_ref, xn_ref, ids_ref, gates_ref, cnt_ref,
                   run_sc, *, n_p, n_rows):
    i = pl.program_id(0)
    tm = xp_ref.shape[0]

    @pl.when(i == 0)
    def _():
        run_sc[...] = jnp.zeros(run_sc.shape, F32)

    x = jnp.where(i < n_p, xp_ref[...], xs_ref[...])
    xn = _rms(x, g_ref[...], RMS_EPS)
    xn_ref[...] = xn
    xh, xl = _split3(xn)
    d = functools.partial(jnp.dot, preferred_element_type=F32)
    logits = d(xh, wh_ref[...]) + d(xl, wh_ref[...]) + d(xh, wl_ref[...]) + b_ref[...]
    lane = lax.broadcasted_iota(jnp.int32, logits.shape, 1)
    ninf = -jnp.inf
    gl = jnp.where(lane < N_EXPERT_GROUPS, logits, ninf)
    gmax = jnp.max(gl, axis=-1, keepdims=True)
    g_val = 1.0 / jnp.sum(jnp.exp(gl - gmax), axis=-1, keepdims=True)
    g_idx = jnp.min(jnp.where(gl == gmax, lane, LANES), axis=-1, keepdims=True)
    lo = N_EXPERT_GROUPS + EXPERTS_PER_GROUP * g_idx
    el = jnp.where((lane >= lo) & (lane < lo + EXPERTS_PER_GROUP), logits, ninf)
    e1 = jnp.max(el, axis=-1, keepdims=True)
    i1 = jnp.min(jnp.where(el == e1, lane, LANES), axis=-1, keepdims=True)
    el2 = jnp.where(lane == i1, ninf, el)
    e2 = jnp.max(el2, axis=-1, keepdims=True)
    i2 = jnp.min(jnp.where(el2 == e2, lane, LANES), axis=-1, keepdims=True)
    r = jnp.exp(e2 - e1)
    w1 = g_val / (1.0 + r)
    w2 = g_val * r / (1.0 + r)
    chose = jnp.where(lane == i1, 1.0, jnp.where(lane == i2, 1.0, 0.0))
    real = (i * tm + lax.broadcasted_iota(jnp.int32, logits.shape, 0)) < n_rows
    chose = jnp.where(real, chose, 0.0)
    before = jnp.dot(tri_ref[...], chose.astype(BF16), preferred_element_type=F32) + run_sc[...]
    r1 = jnp.sum(jnp.where(lane == i1, before, 0.0), axis=-1, keepdims=True).astype(jnp.int32)
    r2 = jnp.sum(jnp.where(lane == i2, before, 0.0), axis=-1, keepdims=True).astype(jnp.int32)
    run_sc[...] = run_sc[...] + jnp.sum(chose, axis=0, keepdims=True)
    ids_ref[...] = jnp.where(lane == 0, i1 - N_EXPERT_GROUPS, jnp.where(lane == 1, i2 - N_EXPERT_GROUPS,
                             jnp.where(lane == 2, r1, jnp.where(lane == 3, r2, 0))))
    gates_ref[...] = jnp.where(lane == 0, w1, jnp.where(lane == 1, w2, 0.0))

    @pl.when(i == pl.num_programs(0) - 1)
    def _():
        cnt_ref[...] = run_sc[...]


def moe_route(xp, xs, g, w_hi, w_lo, bias, tm):
    (tp, d), ts = xp.shape, xs.shape[0]
    n_p, n_s = tp // tm, -(-ts // tm)
    xs = jnp.pad(xs, ((0, n_s * tm - ts), (0, 0)))
    t = tp + n_s * tm
    row = lambda i: (i, 0)
    const = lambda i: (0, 0)
    tri = jnp.tril(jnp.ones((tm, tm), BF16), -1)
    return pl.pallas_call(
        functools.partial(_router_kernel, n_p=n_p, n_rows=tp + ts),
        grid=(n_p + n_s,),
        in_specs=[pl.BlockSpec((tm, d), lambda i: (jnp.minimum(i, n_p - 1), 0)),
                  pl.BlockSpec((tm, d), lambda i: (jnp.maximum(i - n_p, 0), 0)),
                  pl.BlockSpec((1, d), const),
                  pl.BlockSpec((d, LANES), const), pl.BlockSpec((d, LANES), const),
                  pl.BlockSpec((1, LANES), const), pl.BlockSpec((tm, tm), const)],
        out_specs=[pl.BlockSpec((tm, d), row), pl.BlockSpec((tm, LANES), row), pl.BlockSpec((tm, LANES), row),
                   pl.BlockSpec((1, LANES), const)],
        out_shape=[jax.ShapeDtypeStruct((t, d), F32), jax.ShapeDtypeStruct((t, LANES), jnp.int32),
                   jax.ShapeDtypeStruct((t, LANES), F32), jax.ShapeDtypeStruct((1, LANES), F32)],
        scratch_shapes=[pltpu.VMEM((1, LANES), F32)],
        compiler_params=_params(("arbitrary",)),
        name="moe_route",
    )(xp, xs, g, w_hi, w_lo, bias, tri)


SUBLANES = 8
GATHER_UNROLL = 32


def _row_gather_start(idx_ref, base, n, src_hbm, dst, sem):
    tiles = GATHER_UNROLL // SUBLANES

    def body(g, c):
        for j in range(GATHER_UNROLL):
            idx = idx_ref[base + g * GATHER_UNROLL + j]
            pltpu.make_async_copy(src_hbm.at[pl.ds(idx, 1), :],
                                  dst.at[g * tiles + j // SUBLANES, pl.ds(j % SUBLANES, 1), :], sem).start()
        return c
    lax.fori_loop(0, n // GATHER_UNROLL, body, 0)


def _row_gather_wait(dst, sem):
    pltpu.make_async_copy(dst, dst, sem).wait()


W_CHUNK = 256
W_STAGES = 4


def _expert_kernel(bexp_ref, rtok_ref, nused_ref, nxt_ref, c0_ref, c1_ref, slot_ref,
                   xn_hbm, w13_hbm, w2_hbm, ys_ref, xbuf, sem, wb13, wb2, stage, wsem, *, tm, layer):
    rb = pl.program_id(0)
    nused = nused_ref[0]
    n13 = wb13.shape[1] // W_CHUNK
    nc = n13 + wb2.shape[1] // W_CHUNK

    def start_chunk(e, c):
        s = c % W_STAGES

        @pl.when(c < n13)
        def _():
            pltpu.make_async_copy(w13_hbm.at[layer, e, pl.ds(c * W_CHUNK, W_CHUNK), :], stage.at[s], wsem.at[s]).start()

        @pl.when(c >= n13)
        def _():
            pltpu.make_async_copy(w2_hbm.at[layer, e, pl.ds((c - n13) * W_CHUNK, W_CHUNK), :], stage.at[s],
                                  wsem.at[s]).start()

    def finish_chunk(c, dst):
        s = c % W_STAGES
        pltpu.make_async_copy(stage.at[s], stage.at[s], wsem.at[s]).wait()
        v = stage[s].astype(BF16)

        @pl.when(c < n13)
        def _():
            wb13[dst, pl.ds(pl.multiple_of(c * W_CHUNK, W_CHUNK), W_CHUNK), :] = v

        @pl.when(c >= n13)
        def _():
            wb2[dst, pl.ds(pl.multiple_of((c - n13) * W_CHUNK, W_CHUNK), W_CHUNK), :] = v

    def stream(e, lo, hi, dst):
        @pl.when((lo == 0) & (hi > 0))
        def _():
            for c in range(W_STAGES):
                start_chunk(e, jnp.int32(c))

        def body(c, carry):
            finish_chunk(c, dst)

            @pl.when(c + W_STAGES < nc)
            def _():
                start_chunk(e, c + W_STAGES)
            return carry

        lax.fori_loop(lo, hi, body, 0)

    @pl.when(rb == 0)
    def _():
        stream(bexp_ref[0], 0, nc, 0)
        _row_gather_start(rtok_ref, 0, tm, xn_hbm, xbuf.at[0], sem.at[0])

    @pl.when(rb + 1 < nused)
    def _():
        nxt = (rb + 1) % 2
        _row_gather_start(rtok_ref, (rb + 1) * tm, tm, xn_hbm, xbuf.at[nxt], sem.at[nxt])

    @pl.when(rb < nused)
    def _():
        cur = slot_ref[rb]
        stream(nxt_ref[rb], c0_ref[rb], c1_ref[rb], 1 - cur)
        slot = rb % 2
        _row_gather_wait(xbuf.at[slot], sem.at[slot])
        x = xbuf[slot].reshape(tm, xbuf.shape[-1]).astype(BF16)
        h = jnp.dot(x, wb13[cur], preferred_element_type=F32)
        ga, up = h[:, :EXPERT_FF], h[:, EXPERT_FF:]
        a = ga * (1.0 / (1.0 + jnp.exp(-ga))) * up
        ys_ref[...] = jnp.dot(a.astype(BF16), wb2[cur], preferred_element_type=F32)

    @pl.when(rb >= nused)
    def _():
        ys_ref[...] = jnp.zeros(ys_ref.shape, F32)


def moe_experts(xn, w13, w2, layer, block_exp, row_tok, n_used, sched, tm):
    t, d = xn.shape
    n_blocks = block_exp.shape[0]
    anyspace = pl.BlockSpec(memory_space=pl.ANY)
    grid_spec = pltpu.PrefetchScalarGridSpec(
        num_scalar_prefetch=7,
        grid=(n_blocks,),
        in_specs=[anyspace, anyspace, anyspace],
        out_specs=pl.BlockSpec((tm, d), lambda rb, *_: (rb, 0)),
        scratch_shapes=[pltpu.VMEM((2, tm // SUBLANES, SUBLANES, d), F32), pltpu.SemaphoreType.DMA((2,)),
                        pltpu.VMEM((2, d, 2 * EXPERT_FF), BF16), pltpu.VMEM((2, EXPERT_FF, d), BF16),
                        pltpu.VMEM((W_STAGES, W_CHUNK, d), F32), pltpu.SemaphoreType.DMA((W_STAGES,))])
    return pl.pallas_call(
        functools.partial(_expert_kernel, tm=tm, layer=layer),
        grid_spec=grid_spec,
        out_shape=jax.ShapeDtypeStruct((n_blocks * tm, d), F32),
        compiler_params=_params(("arbitrary",)),
        name="moe_experts",
    )(block_exp, row_tok, n_used, *sched, xn, w13, w2)


def _weight_schedule(padded, block_exp, tm, n_chunks):
    e_idx = jnp.arange(N_EXPERTS, dtype=jnp.int32)
    nonempty = padded > 0
    run = jnp.cumsum(nonempty.astype(jnp.int32)) - 1
    later = jnp.where(nonempty, e_idx, N_EXPERTS)
    nxt = jnp.concatenate([lax.cummin(later[::-1])[::-1][1:], jnp.full((1,), N_EXPERTS, jnp.int32)])
    has_next = nxt < N_EXPERTS
    nblk = padded // tm
    first = (jnp.cumsum(padded) - padded) // tm
    e = block_exp
    j = jnp.arange(block_exp.shape[0], dtype=jnp.int32) - first[e]
    n = jnp.maximum(nblk[e], 1)
    live = has_next[e] & (j >= 0) & (j < nblk[e])
    lo = jnp.where(live, j * n_chunks // n, 0)
    hi = jnp.where(live, (j + 1) * n_chunks // n, 0)
    return (jnp.where(has_next[e], nxt[e], e).astype(jnp.int32), lo.astype(jnp.int32), hi.astype(jnp.int32),
            (run[e] % 2).astype(jnp.int32))


def _combine_kernel(pos_ref, ys_hbm, x_ref, gates_ref, o_ref, ybuf, sem, *, tm):
    i = pl.program_id(0)
    n = pl.num_programs(0)

    def start(tile, slot):
        _row_gather_start(pos_ref, tile * 2 * tm, 2 * tm, ys_hbm, ybuf.at[slot], sem.at[slot])

    @pl.when(i == 0)
    def _():
        start(0, 0)

    @pl.when(i + 1 < n)
    def _():
        start(i + 1, (i + 1) % 2)

    slot = i % 2
    _row_gather_wait(ybuf.at[slot], sem.at[slot])
    g = gates_ref[...]
    nt, d = tm // SUBLANES, ybuf.shape[-1]
    y0 = ybuf[slot, :nt].reshape(tm, d)
    y1 = ybuf[slot, nt:].reshape(tm, d)
    o_ref[...] = x_ref[...] + (g[:, 0:1] * y0 + g[:, 1:2] * y1)


def moe_combine(ys, x, gates, pos, tm):
    t, d = x.shape
    pos_tiles = pos.reshape(t // tm, tm, 2).transpose(0, 2, 1).reshape(-1)
    grid_spec = pltpu.PrefetchScalarGridSpec(
        num_scalar_prefetch=1,
        grid=(t // tm,),
        in_specs=[pl.BlockSpec(memory_space=pl.ANY),
                  pl.BlockSpec((tm, d), lambda i, ps: (i, 0)),
                  pl.BlockSpec((tm, LANES), lambda i, ps: (i, 0))],
        out_specs=pl.BlockSpec((tm, d), lambda i, ps: (i, 0)),
        scratch_shapes=[pltpu.VMEM((2, 2 * tm // SUBLANES, SUBLANES, d), F32), pltpu.SemaphoreType.DMA((2,))])
    return pl.pallas_call(
        functools.partial(_combine_kernel, tm=tm),
        grid_spec=grid_spec,
        out_shape=jax.ShapeDtypeStruct((t, d), F32),
        compiler_params=_params(("arbitrary",)),
        name="moe_combine",
    )(pos_tiles, ys, x, gates)


def hier_moe(xp, xs, g, w_group, b_group, w_expert, b_expert, w13, w2, layer, tm_blk=256, tm_tok=128, tm_route=512):
    tp, d = xp.shape
    pad = LANES - N_EXPERT_GROUPS - N_EXPERTS
    w_cat = jnp.concatenate([w_group, w_expert, jnp.zeros((d, pad), F32)], axis=1)
    b_cat = jnp.concatenate([b_group, b_expert, jnp.zeros((pad,), F32)])[None]
    w_hi = w_cat.astype(BF16)
    w_lo = (w_cat - w_hi.astype(F32)).astype(BF16)
    xn, ids, gates, cnt = moe_route(xp, xs, g, w_hi, w_lo, b_cat, tm_route)
    t = tp + xs.shape[0]
    ids = ids[:t]
    n_assign = 2 * t
    counts = cnt[0, N_EXPERT_GROUPS:N_EXPERT_GROUPS + N_EXPERTS].astype(jnp.int32)
    padded = ((counts + tm_blk - 1) // tm_blk) * tm_blk
    pad_end = jnp.cumsum(padded)
    pad_start = pad_end - padded
    pos = pad_start[ids[:, :2]] + ids[:, 2:4]
    n_blocks = -(-n_assign // tm_blk) + N_EXPERTS
    flat_t = jnp.arange(n_assign, dtype=jnp.int32) // 2
    row_tok = jnp.zeros((n_blocks * tm_blk,), jnp.int32).at[pos.reshape(n_assign)].set(flat_t)
    blk_start = jnp.arange(n_blocks, dtype=jnp.int32) * tm_blk
    block_exp = jnp.minimum(jnp.sum((pad_end[None, :] <= blk_start[:, None]).astype(jnp.int32), axis=1), N_EXPERTS - 1)
    n_used = (pad_end[-1:] // tm_blk).astype(jnp.int32)
    n_wchunks = (w13.shape[2] + w2.shape[2]) // W_CHUNK
    sched = _weight_schedule(padded, block_exp, tm_blk, n_wchunks)
    ys = moe_experts(xn, w13, w2, layer, block_exp, row_tok, n_used, sched, tm_blk)
    return (moe_combine(ys, xp, gates[:tp], pos[:tp], tm_tok),
            moe_combine(ys, xs, gates[tp:t], pos[tp:], min(tm_tok, xs.shape[0])))


def _cmul(ar, ai, br, bi):
    return ar * br - ai * bi, ar * bi + ai * br


def _ssm_prep_kernel(are_ref, aim_ref, ldt_ref, bre_ref, bim_ref, cre_ref, cim_ref,
                     m_ref, bc_ref, cc_ref, a16_ref, a4_ref, *, n_new):
    p = SSM_STATE
    nc = SSM_CHUNK
    a_re, a_im = are_ref[...], aim_ref[...]
    dt = jnp.exp(ldt_ref[...])
    mag = jnp.exp(dt * a_re)
    ab_re = mag * jnp.cos(dt * a_im)
    ab_im = mag * jnp.sin(dt * a_im)
    den = a_re * a_re + a_im * a_im
    nr = ab_re - 1.0
    fr = (nr * a_re + ab_im * a_im) / den
    fi = (ab_im * a_re - nr * a_im) / den
    bt_re, bt_im = bre_ref[...], bim_ref[...]
    bb_re = fr * bt_re - fi * bt_im
    bb_im = fr * bt_im + fi * bt_re
    c_re, c_im = cre_ref[...], cim_ref[...]
    pw = [(jnp.ones_like(ab_re), jnp.zeros_like(ab_re))]
    for _ in range(nc):
        pw.append(_cmul(pw[-1][0], pw[-1][1], ab_re, ab_im))
    wst = []
    for tau in range(nc + 1):
        wr, wi = _cmul(c_re, c_im, pw[tau][0], pw[tau][1])
        wst.append(jnp.concatenate([wr, -wi], axis=1))
    wstack = jnp.concatenate(wst[:nc], axis=0)
    bs_t = jnp.concatenate([bb_re, bb_im], axis=1)
    q = _dot3(bs_t, wstack, _NT)
    qpad = jnp.concatenate([jnp.zeros_like(q), q], axis=1)
    w = nc * SSM_GROUP_CH
    rows = []
    for s in range(nc):
        off = (nc - s) * SSM_GROUP_CH
        rows.append(qpad[:, off:off + w])
    m_ref[...] = jnp.concatenate(rows, axis=0).astype(m_ref.dtype)
    bc = []
    for s in range(nc):
        xr, xi = _cmul(bb_re, bb_im, pw[nc - 1 - s][0], pw[nc - 1 - s][1])
        bc.append(jnp.concatenate([xr, xi], axis=1))
    bc_ref[...] = jnp.concatenate(bc, axis=0).astype(bc_ref.dtype)
    cc_ref[...] = jnp.concatenate(wst[1:], axis=0).astype(cc_ref.dtype)
    a16_ref[...] = jnp.concatenate([pw[nc][0], pw[nc][1]], axis=1)
    a4_ref[...] = jnp.concatenate([pw[n_new][0], pw[n_new][1]], axis=1)


def ssm_prep(a_re, a_im, log_dt, b_re, b_im, c_re, c_im, n_new):
    g, p = a_re.shape
    gc = SSM_GROUP_CH
    w = SSM_CHUNK * gc
    vec = pl.BlockSpec((None, 1, p), lambda i: (i, 0, 0))
    mat = pl.BlockSpec((None, gc, p), lambda i: (i, 0, 0))
    bt_re = b_re.transpose(0, 2, 1)
    bt_im = b_im.transpose(0, 2, 1)
    return pl.pallas_call(
        functools.partial(_ssm_prep_kernel, n_new=n_new),
        grid=(g,),
        in_specs=[vec, vec, pl.BlockSpec((None, 1, 1), lambda i: (i, 0, 0)), mat, mat, mat, mat],
        out_specs=[pl.BlockSpec((None, w, w), lambda i: (i, 0, 0)),
                   pl.BlockSpec((None, w, 2 * p), lambda i: (i, 0, 0)),
                   pl.BlockSpec((None, w, 2 * p), lambda i: (i, 0, 0)),
                   pl.BlockSpec((None, 1, 2 * p), lambda i: (i, 0, 0)),
                   pl.BlockSpec((None, 1, 2 * p), lambda i: (i, 0, 0))],
        out_shape=[jax.ShapeDtypeStruct((g, w, w), BF16),
                   jax.ShapeDtypeStruct((g, w, 2 * p), BF16),
                   jax.ShapeDtypeStruct((g, w, 2 * p), BF16),
                   jax.ShapeDtypeStruct((g, 1, 2 * p), F32),
                   jax.ShapeDtypeStruct((g, 1, 2 * p), F32)],
        compiler_params=_params(("arbitrary",)),
        name="ssm_prep",
    )(a_re[:, None], a_im[:, None], log_dt[:, None, None], bt_re, bt_im, c_re, c_im)


GROUPS_PER_VREG = LANES // SSM_GROUP_CH
PERM_BATCH = 8


def _chunk_perm():
    n = GROUPS_PER_VREG
    idx = np.arange(n * LANES)
    a, b, c = idx // LANES, (idx // SSM_GROUP_CH) % n, idx % SSM_GROUP_CH
    m = np.zeros((n * LANES, n * LANES), np.float32)
    m[idx, (b * n + a) * SSM_GROUP_CH + c] = 1.0
    return jnp.asarray(m, BF16)


def _ssm_in_kernel(*refs):
    nj = (len(refs) - 3) // 2
    x_refs, g_refs = refs[:nj], refs[nj:2 * nj]
    p_ref, u_ref, inv_sc = refs[2 * nj], refs[2 * nj + 1], refs[2 * nj + 2]
    tm = x_refs[0].shape[0]
    nc, n = tm // SSM_CHUNK, GROUPS_PER_VREG
    ssq = jnp.zeros((tm, LANES), F32)
    for x_ref in x_refs:
        xj = x_ref[...]
        ssq = ssq + xj * xj
    inv = lax.rsqrt(jnp.sum(ssq, axis=-1, keepdims=True) * (1.0 / (nj * LANES)) + RMS_EPS)
    inv_sc[...] = jnp.broadcast_to(inv, (tm, LANES))

    def piece(j, s):
        rows = pl.ds(s, nc, stride=SSM_CHUNK)
        return x_refs[j][rows, :] * inv_sc[rows, :] * g_refs[j][...]

    pairs = [(j, h) for j in range(nj) for h in range(SSM_CHUNK // n)]
    for p0 in range(0, len(pairs), PERM_BATCH):
        batch = pairs[p0:p0 + PERM_BATCH]
        lhs = jnp.concatenate(
            [jnp.concatenate([piece(j, h * n + sl) for sl in range(n)], axis=1).astype(BF16) for j, h in batch],
            axis=0)
        r = jnp.dot(lhs, p_ref[...], preferred_element_type=F32)
        for k, (j, h) in enumerate(batch):
            for gl in range(n):
                u_ref[j * n + gl, :, h * LANES:(h + 1) * LANES] = (
                    r[k * nc:(k + 1) * nc, gl * LANES:(gl + 1) * LANES].astype(BF16))


def ssm_chunk_in(x, g, perm, tm):
    t, d = x.shape
    ng, w = d // SSM_GROUP_CH, SSM_CHUNK * SSM_GROUP_CH
    nc, nj = tm // SSM_CHUNK, d // LANES
    return pl.pallas_call(
        _ssm_in_kernel,
        grid=(t // tm,),
        in_specs=[pl.BlockSpec((tm, LANES), lambda i, j=j: (i, j)) for j in range(nj)]
                 + [pl.BlockSpec((1, LANES), lambda i, j=j: (0, j)) for j in range(nj)]
                 + [pl.BlockSpec(perm.shape, lambda i: (0, 0))],
        out_specs=pl.BlockSpec((ng, nc, w), lambda i: (0, i, 0)),
        out_shape=jax.ShapeDtypeStruct((ng, t // SSM_CHUNK, w), BF16),
        scratch_shapes=[pltpu.VMEM((tm, LANES), F32)],
        compiler_params=_params(("arbitrary",)),
        name="ssm_chunk_in",
    )(*([x] * nj), *([g] * nj), perm)


def _ssm_out_kernel(z_ref, p_ref, o_ref, scr):
    tm, d = o_ref.shape
    nc, nj, n = tm // SSM_CHUNK, d // LANES, GROUPS_PER_VREG
    pairs = [(j, h) for j in range(nj) for h in range(SSM_CHUNK // n)]
    for p0 in range(0, len(pairs), PERM_BATCH):
        batch = pairs[p0:p0 + PERM_BATCH]
        lhs = jnp.concatenate(
            [jnp.concatenate([z_ref[j * n + gl, :, h * LANES:(h + 1) * LANES] for gl in range(n)], axis=1)
             for j, h in batch], axis=0)
        r = jnp.dot(lhs, p_ref[...], preferred_element_type=F32)
        for k, (j, h) in enumerate(batch):
            for tl in range(n):
                scr[pl.ds((h * n + tl) * nj + j, nc, stride=SSM_CHUNK * nj), :] = (
                    r[k * nc:(k + 1) * nc, tl * LANES:(tl + 1) * LANES])
    for j in range(nj):
        o_ref[:, j * LANES:(j + 1) * LANES] = scr[pl.ds(j, tm, stride=nj), :].astype(o_ref.dtype)


def ssm_chunk_out(z, perm, tm, d):
    ng, n_c, w = z.shape
    nc = tm // SSM_CHUNK
    return pl.pallas_call(
        _ssm_out_kernel,
        grid=(n_c // nc,),
        in_specs=[pl.BlockSpec((ng, nc, w), lambda i: (0, i, 0)), pl.BlockSpec(perm.shape, lambda i: (0, 0))],
        out_specs=pl.BlockSpec((tm, d), lambda i: (i, 0)),
        out_shape=jax.ShapeDtypeStruct((n_c * SSM_CHUNK, d), z.dtype),
        scratch_shapes=[pltpu.VMEM((tm * (d // LANES), LANES), F32)],
        compiler_params=_params(("arbitrary",)),
        name="ssm_chunk_out",
    )(z, perm)


def _ssm_x_kernel(u_ref, bc_ref, x_ref, *, k):
    for gi in range(u_ref.shape[0]):
        bc = bc_ref[gi]
        x_ref[gi] = jnp.dot(u_ref[gi, :, :k], bc[bc.shape[0] - k:], preferred_element_type=F32)


def _ssm_scan_kernel(xp_ref, xs_ref, h0_ref, a16_ref, a4_ref, hs_ref, hfin_ref, hsfin_ref, *, n_chunks, batch):
    p = SSM_STATE

    def coef(a):
        ar, ai = a[:, :, :p], a[:, :, p:]
        return jnp.concatenate([ar, ar], axis=-1), jnp.concatenate([-ai, ai], axis=-1)

    def step(h, c1, c2):
        return c1 * h + c2 * pltpu.roll(h, p, 2)

    c1, c2 = coef(a16_ref[...])

    def body(c, h):
        rows = pl.ds(c, batch, stride=n_chunks)
        hs_ref[:, rows, :] = h
        return step(h, c1, c2) + xp_ref[:, rows, :]

    h0 = jnp.zeros((xp_ref.shape[0], batch, 2 * p), F32)
    hfin_ref[...] = lax.fori_loop(0, n_chunks, body, h0)
    d1, d2 = coef(a4_ref[...])
    hsfin_ref[...] = step(h0_ref[...], d1, d2) + xs_ref[...]


def _ssm_y_kernel(u_ref, hs_ref, m_ref, cc_ref, d_ref, z_ref):
    c = math.sqrt(2.0 / math.pi)
    for gi in range(u_ref.shape[0]):
        u = u_ref[gi]
        y = jnp.dot(u, m_ref[gi], preferred_element_type=F32)
        y = y + lax.dot_general(hs_ref[gi].astype(BF16), cc_ref[gi], (_NT, ((), ())), preferred_element_type=F32)
        y = y + d_ref[gi] * u.astype(F32)
        z = 0.5 * y * (1.0 + jnp.tanh(c * (y + 0.044715 * (y * y * y))))
        z_ref[gi] = z.astype(z_ref.dtype)


def ssm_mix(up, us, h0_g, ops, d_tile, n_chunks, batch, n_new, g_blk=8, g_scan=16):
    m_g, bc_g, cc_g, a16, a4 = ops
    g, _, w = up.shape
    p2 = 2 * SSM_STATE
    n_s = us.shape[1]

    def blk(gb, *shape):
        return pl.BlockSpec((gb,) + shape, lambda i: (i,) + (0,) * len(shape))

    def chunk_state(u, k):
        r = u.shape[1]
        return pl.pallas_call(
            functools.partial(_ssm_x_kernel, k=k),
            grid=(g // g_blk,),
            in_specs=[blk(g_blk, r, w), blk(g_blk, w, p2)],
            out_specs=blk(g_blk, r, p2),
            out_shape=jax.ShapeDtypeStruct((g, r, p2), F32),
            compiler_params=_params(("arbitrary",)),
            name="ssm_chunk_state",
        )(u, bc_g)

    def output(u, hs):
        r = u.shape[1]
        return pl.pallas_call(
            _ssm_y_kernel,
            grid=(g // g_blk,),
            in_specs=[blk(g_blk, r, w), blk(g_blk, r, p2), blk(g_blk, w, w), blk(g_blk, w, p2), blk(g_blk, 1, w)],
            out_specs=blk(g_blk, r, w),
            out_shape=jax.ShapeDtypeStruct((g, r, w), BF16),
            compiler_params=_params(("arbitrary",)),
            name="ssm_output",
        )(u, hs, m_g, cc_g, d_tile)

    xp = chunk_state(up, w)
    xs = chunk_state(us, n_new * SSM_GROUP_CH)
    r = up.shape[1]
    hs, hfin, hsfin = pl.pallas_call(
        functools.partial(_ssm_scan_kernel, n_chunks=n_chunks, batch=batch),
        grid=(g // g_scan,),
        in_specs=[blk(g_scan, r, p2), blk(g_scan, n_s, p2), blk(g_scan, n_s, p2), blk(g_scan, 1, p2),
                  blk(g_scan, 1, p2)],
        out_specs=[blk(g_scan, r, p2), blk(g_scan, batch, p2), blk(g_scan, n_s, p2)],
        out_shape=[jax.ShapeDtypeStruct((g, r, p2), F32),
                   jax.ShapeDtypeStruct((g, batch, p2), F32),
                   jax.ShapeDtypeStruct((g, n_s, p2), F32)],
        compiler_params=_params(("arbitrary",)),
        name="ssm_scan",
    )(xp, xs, h0_g, a16, a4)
    return output(up, hs), output(us, h0_g), hfin, hsfin


def kernel(x_prompt, x_sample, cache_k, cache_v, state_ssm_re, state_ssm_im, page_table, norm_mix, norm_ffn, norm_final, attn_w_qkv, attn_lambda_q1, attn_lambda_k1, attn_lambda_q2, attn_lambda_k2, attn_subln, attn_w_o, ssm_A_re, ssm_A_im, ssm_log_dt, ssm_B_re, ssm_B_im, ssm_C_re, ssm_C_im, ssm_D, ssm_glu_w, ssm_glu_b, moe_w_group, moe_b_group, moe_w_expert, moe_b_expert, moe_w13, moe_w2):
    batch, seq, d = x_prompt.shape
    dec_b, dec_s, _ = x_sample.shape
    depth = norm_mix.shape[0]
    tp, ts = batch * seq, dec_b * dec_s
    xp = x_prompt.reshape(tp, d)
    xs = x_sample.reshape(ts, d)
    g, pst, gc = ssm_B_re.shape[1:]
    n_chunks = seq // SSM_CHUNK
    k_p, v_p, k_s, v_s = [], [], [], []
    hr_p, hi_p, hr_s, hi_s = [], [], [], []
    for i in range(depth):
        gm = norm_mix[i][None]
        if i % 2 == 0:
            a = i // 2
            lam_init = 0.8 - 0.6 * math.exp(-0.3 * i)
            lams = (attn_lambda_q1[a][None], attn_lambda_k1[a][None], attn_lambda_q2[a][None], attn_lambda_k2[a][None])
            subln = attn_subln[a][None]
            w_qkv = attn_w_qkv[a].astype(BF16)
            w_o = attn_w_o[a].astype(BF16)
            qp, khp, kp, vp = qkv_proj(xp, gm, w_qkv, 1024, 512)
            qs, _, ks, vs = qkv_proj(xs, gm, w_qkv, ts, 512)
            op = attn_prompt(qp, khp, vp, lams, subln, batch, seq, lam_init)
            os_ = attn_sample(qs.reshape(dec_b, dec_s, d), ks.reshape(dec_b, dec_s * 2 * N_HEADS, HEAD_DIM),
                              vs.reshape(dec_b, dec_s, d), cache_k, cache_v, page_table, lams, subln, a, lam_init)
            xp = matmul_residual(op, w_o, xp, 1024, 512)
            xs = matmul_residual(os_.reshape(ts, d), w_o, xs, ts, 512)
            k_p.append(kp.reshape(batch, seq, N_HEADS, 2, HEAD_DIM))
            v_p.append(vp.reshape(batch, seq, N_HEADS, HEAD_W))
            k_s.append(ks.reshape(dec_b, dec_s, N_HEADS, 2, HEAD_DIM))
            v_s.append(vs.reshape(dec_b, dec_s, N_HEADS, HEAD_W))
        else:
            s = i // 2
            ops = ssm_prep(ssm_A_re[s], ssm_A_im[s], ssm_log_dt[s], ssm_B_re[s], ssm_B_im[s],
                           ssm_C_re[s], ssm_C_im[s], dec_s)
            perm = _chunk_perm()
            up = ssm_chunk_in(xp, gm, perm, 1024)
            hs = rms_norm(xs, gm, ts)
            us = hs.reshape(dec_b, dec_s, g, gc).transpose(2, 0, 1, 3).reshape(g, dec_b, dec_s * gc)
            us = jnp.pad(us, ((0, 0), (0, 0), (0, (SSM_CHUNK - dec_s) * gc))).astype(BF16)
            h0_g = jnp.concatenate([state_ssm_re[s], state_ssm_im[s]], axis=-1).transpose(1, 0, 2)
            d_tile = jnp.tile(ssm_D[s], (1, SSM_CHUNK))[:, None]
            zp_g, zs_g, hfin, hsfin = ssm_mix(up, us, h0_g, ops, d_tile, n_chunks, batch, dec_s)
            zp = ssm_chunk_out(zp_g, perm, 1024, d)
            zs = zs_g[:, :, :dec_s * gc].reshape(g, dec_b, dec_s, gc).transpose(1, 2, 0, 3).reshape(ts, d)
            glu_w = ssm_glu_w[s].astype(BF16)
            glu_b = ssm_glu_b[s][None]
            xp = glu_residual(zp, glu_w, glu_b, xp, 1024, 512)
            xs = glu_residual(zs, glu_w, glu_b, xs, ts, 512)
            hfin = hfin.transpose(1, 0, 2)
            hsfin = hsfin.transpose(1, 0, 2)
            hr_p.append(hfin[..., :pst])
            hi_p.append(hfin[..., pst:])
            hr_s.append(hsfin[..., :pst])
            hi_s.append(hsfin[..., pst:])
        xp, xs = hier_moe(xp, xs, norm_ffn[i][None], moe_w_group[i], moe_b_group[i], moe_w_expert[i],
                          moe_b_expert[i], moe_w13, moe_w2, i)
    y_prompt = rms_norm(xp, norm_final[None], 1024).reshape(batch, seq, d)
    y_sample = rms_norm(xs, norm_final[None], ts).reshape(dec_b, dec_s, d)
    return (y_prompt, y_sample, jnp.stack(k_p), jnp.stack(v_p), jnp.stack(k_s), jnp.stack(v_s),
            jnp.stack(hr_p), jnp.stack(hi_p), jnp.stack(hr_s), jnp.stack(hi_s))
```

```python
import functools
import math

import numpy as np
import jax
import jax.numpy as jnp
from jax import lax
from jax.experimental import pallas as pl
from jax.experimental.pallas import tpu as pltpu

F32 = jnp.float32
BF16 = jnp.bfloat16

RMS_EPS = 1e-6
SUBLN_EPS = 1e-5
HEAD_DIM = 128
HEAD_W = 2 * HEAD_DIM
N_HEADS = 8
Q_PRESCALE = HEAD_DIM ** -0.5 * math.log2(math.e)
PAGE_SIZE = 128
N_EXPERT_GROUPS = 4
EXPERTS_PER_GROUP = 8
N_EXPERTS = N_EXPERT_GROUPS * EXPERTS_PER_GROUP
EXPERT_FF = 1024
SSM_GROUP_CH = 16
SSM_STATE = 64
SSM_CHUNK = 16
LANES = 128
VMEM_LIMIT = 56 * 1024 * 1024


def _params(sem, vmem=VMEM_LIMIT):
    return pltpu.CompilerParams(dimension_semantics=sem, vmem_limit_bytes=vmem)


def _rms(x, g, eps):
    return x * lax.rsqrt(jnp.mean(x * x, axis=-1, keepdims=True) + eps) * g


def _split3(x):
    hi = x.astype(BF16)
    lo = (x - hi.astype(F32)).astype(BF16)
    return hi, lo


def _dot3(a, b, dims):
    ah, al = _split3(a)
    bh, bl = _split3(b)
    dn = (dims, ((), ()))
    d = functools.partial(lax.dot_general, dimension_numbers=dn, preferred_element_type=F32)
    return d(ah, bh) + d(al, bh) + d(ah, bl)


_NT = ((1,), (1,))
_NN = ((1,), (0,))


def _qkv_kernel(x_ref, g_ref, w_ref, q_ref, kh_ref, kn_ref, v_ref, xn_ref, *, nj):
    j = pl.program_id(1)
    tn = w_ref.shape[1]

    @pl.when(j == 0)
    def _():
        xn_ref[...] = _rms(x_ref[...], g_ref[...], RMS_EPS).astype(BF16)

    acc = jnp.dot(xn_ref[...], w_ref[...], preferred_element_type=F32)

    @pl.when(j < nj)
    def _():
        q_ref[...] = (acc * Q_PRESCALE).astype(q_ref.dtype)

    for jj in range(nj):
        @pl.when(j == nj + jj)
        def _(jj=jj):
            kh_ref[...] = acc.astype(kh_ref.dtype)
            for i in range(tn // HEAD_DIM):
                kn_ref[:, jj * (tn // HEAD_DIM) + i, :] = acc[:, i * HEAD_DIM:(i + 1) * HEAD_DIM]

    @pl.when(j >= 2 * nj)
    def _():
        v_ref[...] = acc


def qkv_proj(x, g, w_bf, tm, tn):
    t, d = x.shape
    nj = d // tn

    def omap(part):
        return lambda i, j: (i, jnp.clip(j - part * nj, 0, nj - 1))

    return pl.pallas_call(
        functools.partial(_qkv_kernel, nj=nj),
        grid=(t // tm, 3 * nj),
        in_specs=[pl.BlockSpec((tm, d), lambda i, j: (i, 0)),
                  pl.BlockSpec((1, d), lambda i, j: (0, 0)),
                  pl.BlockSpec((d, tn), lambda i, j: (0, j))],
        out_specs=[pl.BlockSpec((tm, tn), omap(0)), pl.BlockSpec((tm, tn), omap(1)),
                   pl.BlockSpec((tm, d // HEAD_DIM, HEAD_DIM), lambda i, j: (i, 0, 0)),
                   pl.BlockSpec((tm, tn), omap(2))],
        out_shape=[jax.ShapeDtypeStruct((t, d), BF16), jax.ShapeDtypeStruct((t, d), BF16),
                   jax.ShapeDtypeStruct((t, d // HEAD_DIM, HEAD_DIM), F32), jax.ShapeDtypeStruct((t, d), F32)],
        scratch_shapes=[pltpu.VMEM((tm, d), BF16)],
        compiler_params=_params(("arbitrary", "arbitrary")),
        name="qkv_proj",
    )(x, g, w_bf)


def _mm_res_kernel(a_ref, w_ref, x_ref, o_ref):
    o_ref[...] = x_ref[...] + jnp.dot(a_ref[...].astype(BF16), w_ref[...], preferred_element_type=F32)


def matmul_residual(a, w_bf, x, tm, tn):
    t, kd = a.shape
    n = w_bf.shape[1]
    return pl.pallas_call(
        _mm_res_kernel,
        grid=(t // tm, n // tn),
        in_specs=[pl.BlockSpec((tm, kd), lambda i, j: (i, 0)),
                  pl.BlockSpec((kd, tn), lambda i, j: (0, j)),
                  pl.BlockSpec((tm, tn), lambda i, j: (i, j))],
        out_specs=pl.BlockSpec((tm, tn), lambda i, j: (i, j)),
        out_shape=jax.ShapeDtypeStruct((t, n), F32),
        compiler_params=_params(("arbitrary", "arbitrary")),
        name="matmul_residual",
    )(a, w_bf, x)


def _glu_kernel(z_ref, w1_ref, w2_ref, b1_ref, b2_ref, x_ref, o_ref, zb):
    @pl.when(pl.program_id(1) == 0)
    def _():
        for j in range(z_ref.shape[0]):
            zb[:, j * LANES:(j + 1) * LANES] = z_ref[j].astype(BF16)

    z = zb[...]
    a = jnp.dot(z, w1_ref[...], preferred_element_type=F32) + b1_ref[...]
    b = jnp.dot(z, w2_ref[...], preferred_element_type=F32) + b2_ref[...]
    o_ref[...] = x_ref[...] + a * (1.0 / (1.0 + jnp.exp(-b)))


def glu_residual(z, w_bf, bias, x, tm, tn):
    nb, t, _ = z.shape
    d = nb * LANES
    nj = d // tn
    return pl.pallas_call(
        _glu_kernel,
        grid=(t // tm, nj),
        in_specs=[pl.BlockSpec((nb, tm, LANES), lambda i, j: (0, i, 0)),
                  pl.BlockSpec((d, tn), lambda i, j: (0, j)),
                  pl.BlockSpec((d, tn), lambda i, j: (0, j + nj)),
                  pl.BlockSpec((1, tn), lambda i, j: (0, j)),
                  pl.BlockSpec((1, tn), lambda i, j: (0, j + nj)),
                  pl.BlockSpec((tm, tn), lambda i, j: (i, j))],
        out_specs=pl.BlockSpec((tm, tn), lambda i, j: (i, j)),
        out_shape=jax.ShapeDtypeStruct((t, d), F32),
        scratch_shapes=[pltpu.VMEM((tm, d), BF16)],
        compiler_params=_params(("arbitrary", "arbitrary")),
        name="glu_residual",
    )(z, w_bf, w_bf, bias, bias, x)


def _norm_kernel(x_ref, g_ref, o_ref):
    o_ref[...] = _rms(x_ref[...], g_ref[...], RMS_EPS).astype(o_ref.dtype)


def rms_norm(x, g, tm):
    t, d = x.shape
    return pl.pallas_call(
        _norm_kernel,
        grid=(t // tm,),
        in_specs=[pl.BlockSpec((tm, d), lambda i: (i, 0)),
                  pl.BlockSpec((1, d), lambda i: (0, 0))],
        out_specs=pl.BlockSpec((tm, d), lambda i: (i, 0)),
        out_shape=jax.ShapeDtypeStruct((t, d), F32),
        compiler_params=_params(("arbitrary",)),
        name="rms_norm",
    )(x, g)


def _diff_lambda(lq1, lk1, lq2, lk2, lam_init):
    return (jnp.exp(jnp.sum(lq1[...] * lk1[...], axis=-1, keepdims=True))
            - jnp.exp(jnp.sum(lq2[...] * lk2[...], axis=-1, keepdims=True)) + lam_init)


def _attn_prompt_kernel(q_ref, k_ref, v_ref, lq1, lk1, lq2, lk2, subln_ref, o_ref,
                        vtb, acc_sc, *, tq, lam_init):
    seq = q_ref.shape[0]
    for j in range(seq // tq):
        vtb[j] = v_ref[j * tq:(j + 1) * tq, :].T.astype(BF16)
    lam = _diff_lambda(lq1, lk1, lq2, lk2, lam_init)
    krow = lax.broadcasted_iota(jnp.int32, (tq, tq), 0)
    qcol = lax.broadcasted_iota(jnp.int32, (tq, tq), 1)
    causal = krow <= qcol

    def q_body(qi, carry):
        q0 = pl.multiple_of(qi * tq, tq)
        qt = q_ref[pl.ds(q0, tq), :]
        acc_sc[...] = jnp.zeros(acc_sc.shape, F32)

        def kv_step(kj, masked, st):
            k0 = pl.multiple_of(kj * tq, tq)
            kt = k_ref[pl.ds(k0, tq), :]
            vt = vtb[kj]
            out = []
            for s in range(2):
                ls = slice(s * HEAD_DIM, (s + 1) * HEAD_DIM)
                sc = lax.dot_general(kt[:, ls], qt[:, ls], (_NT, ((), ())), preferred_element_type=F32)
                if masked:
                    sc = jnp.where(causal, sc, -jnp.inf)
                m_prev, l_prev = st[2 * s], st[2 * s + 1]
                m_new = jnp.maximum(m_prev, jnp.max(sc, axis=0, keepdims=True))
                alpha = jnp.exp2(m_prev - m_new)
                p = jnp.exp2(sc - m_new)
                out += [m_new, alpha * l_prev + jnp.sum(p, axis=0, keepdims=True)]
                acc_sc[s] = alpha * acc_sc[s] + jnp.dot(vt, p.astype(BF16), preferred_element_type=F32)
            return tuple(out)

        ninf = jnp.full((1, tq), -jnp.inf, F32)
        zero = jnp.zeros((1, tq), F32)
        st = lax.fori_loop(0, qi, lambda kj, st: kv_step(kj, False, st), (ninf, zero, ninf, zero))
        _, l0, _, l1 = kv_step(qi, True, st)
        ot = acc_sc[0] / l0 - lam * (acc_sc[1] / l1)
        ot = ot * lax.rsqrt(jnp.mean(ot * ot, axis=0, keepdims=True) + SUBLN_EPS)
        o_ref[pl.ds(q0, tq), :] = (ot.T * subln_ref[...] * (1.0 - lam_init)).astype(o_ref.dtype)
        return carry

    lax.fori_loop(0, seq // tq, q_body, 0)


def attn_prompt(q, k, v, lams, subln, batch, seq, lam_init, tq=512):
    t, d = q.shape
    blk = pl.BlockSpec((seq, HEAD_W), lambda b, h: (b, h))
    vec = pl.BlockSpec((1, HEAD_DIM), lambda b, h: (0, 0))
    return pl.pallas_call(
        functools.partial(_attn_prompt_kernel, tq=tq, lam_init=lam_init),
        grid=(batch, N_HEADS),
        in_specs=[blk, blk, blk, vec, vec, vec, vec,
                  pl.BlockSpec((1, HEAD_W), lambda b, h: (0, 0))],
        out_specs=blk,
        out_shape=jax.ShapeDtypeStruct((t, d), BF16),
        scratch_shapes=[pltpu.VMEM((seq // tq, HEAD_W, tq), BF16), pltpu.VMEM((2, HEAD_W, tq), F32)],
        compiler_params=_params(("arbitrary", "arbitrary")),
        name="attn_prompt",
    )(q, k, v, *lams, subln)


def _attn_sample_kernel(pt_ref, q_ref, kn_ref, vn_ref, *rest, n_new, n_pp, lam_init):
    del pt_ref
    kp_refs, vp_refs = rest[:n_pp], rest[n_pp:2 * n_pp]
    bias_ref, biasn_ref, lq1, lk1, lq2, lk2, subln_ref, o_ref, pbuf, m_sc, l_sc, acc_sc = rest[2 * n_pp:]
    p = pl.program_id(1)
    qrows = q_ref[...].astype(BF16)

    @pl.when(p == 0)
    def _():
        m_sc[...] = jnp.full(m_sc.shape, -jnp.inf, F32)
        l_sc[...] = jnp.zeros(l_sc.shape, F32)
        acc_sc[...] = jnp.zeros(acc_sc.shape, F32)

    def attend(k_rows, v_rows, bias):
        r = k_rows.shape[0]
        st = lax.dot_general(k_rows.astype(BF16), qrows, (_NT, ((), ())), preferred_element_type=F32)
        st = st.reshape(r // bias.shape[0], bias.shape[0], LANES) + bias[None]
        m_prev = m_sc[...]
        m_new = jnp.maximum(m_prev, jnp.max(jnp.max(st, axis=0), axis=0, keepdims=True))
        alpha = jnp.exp2(m_prev - m_new)
        pr = jnp.exp2(st - m_new)
        l_sc[...] = alpha * l_sc[...] + jnp.sum(jnp.sum(pr, axis=0), axis=0, keepdims=True)
        m_sc[...] = m_new
        pbuf[0:r, :] = pr.reshape(r, LANES)
        pv = pbuf[pl.ds(0, r // 2, stride=2), :] + pbuf[pl.ds(1, r // 2, stride=2), :]
        vt = v_rows.T.astype(BF16)
        acc_sc[...] = alpha * acc_sc[...] + jnp.dot(vt, pv.astype(BF16), preferred_element_type=F32)

    for kp_ref, vp_ref in zip(kp_refs, vp_refs):
        attend(kp_ref[...], vp_ref[...], bias_ref[...])

    @pl.when(p == pl.num_programs(1) - 1)
    def _():
        attend(kn_ref[...], vn_ref[...], biasn_ref[...])
        lam = _diff_lambda(lq1, lk1, lq2, lk2, lam_init)
        on = acc_sc[...] / l_sc[...]
        d = on - lam * pltpu.roll(on, LANES - n_new, 1)
        d = d * lax.rsqrt(jnp.mean(d * d, axis=0, keepdims=True) + SUBLN_EPS)
        o = d.T * subln_ref[...] * (1.0 - lam_init)
        for h in range(N_HEADS):
            o_ref[:, h * HEAD_W:(h + 1) * HEAD_W] = o[h * 2 * n_new:h * 2 * n_new + n_new]


def attn_sample(q, k_new, v_new, cache_k, cache_v, page_table, lams, subln, layer, lam_init, n_pp=8):
    b, n_new, d = q.shape
    n_pages = page_table.shape[1]
    n_phys = cache_k.shape[1]
    kh = 2 * N_HEADS
    nrow = N_HEADS * 2 * n_new
    n_slot = 16
    assert nrow <= LANES and n_new <= n_slot
    k_view = cache_k.reshape(cache_k.shape[0] * n_phys, PAGE_SIZE * kh, HEAD_DIM)
    v_view = cache_v.reshape(cache_v.shape[0] * n_phys, PAGE_SIZE * N_HEADS, HEAD_W)
    qrows = q.reshape(b, n_new, kh, HEAD_DIM).transpose(0, 2, 1, 3).reshape(b, nrow, HEAD_DIM)
    qrows = jnp.pad(qrows, ((0, 0), (0, LANES - nrow), (0, 0)))
    kn = jnp.pad(k_new, ((0, 0), (0, (n_slot - n_new) * kh), (0, 0)))
    vn = jnp.pad(v_new, ((0, 0), (0, n_slot - n_new), (0, 0))).reshape(b, n_slot * N_HEADS, HEAD_W)
    lane = jnp.arange(LANES)
    own = (jnp.arange(kh)[:, None] == (lane // n_new)[None, :]) | (lane >= nrow)[None, :]
    bias = jnp.where(own, 0.0, -jnp.inf).astype(F32)
    tokn = jnp.arange(n_slot)[:, None, None]
    seen = (tokn < n_new) & (tokn <= (lane % n_new)[None, None, :])
    bias_new = jnp.where((own[None] & seen) | (lane >= nrow)[None, None, :], 0.0, -jnp.inf)
    bias_new = bias_new.astype(F32).reshape(n_slot * kh, LANES)
    tok = pl.BlockSpec((None, n_new, d), lambda i, p, pt: (i, 0, 0))

    def per_seq(rows, width):
        return pl.BlockSpec((None, rows, width), lambda i, p, pt: (i, 0, 0))

    def page_spec(rows, width, k):
        return pl.BlockSpec((None, rows, width),
                            lambda i, p, pt: (layer * n_phys + pt[i * n_pages + p * n_pp + k], 0, 0))

    def const(rows, width):
        return pl.BlockSpec((rows, width), lambda i, p, pt: (0, 0))

    grid_spec = pltpu.PrefetchScalarGridSpec(
        num_scalar_prefetch=1,
        grid=(b, n_pages // n_pp),
        in_specs=[per_seq(LANES, HEAD_DIM), per_seq(n_slot * kh, HEAD_DIM), per_seq(n_slot * N_HEADS, HEAD_W)]
                 + [page_spec(PAGE_SIZE * kh, HEAD_DIM, k) for k in range(n_pp)]
                 + [page_spec(PAGE_SIZE * N_HEADS, HEAD_W, k) for k in range(n_pp)]
                 + [const(kh, LANES), const(n_slot * kh, LANES)]
                 + [const(1, HEAD_DIM)] * 4 + [const(1, HEAD_W)],
        out_specs=tok,
        scratch_shapes=[pltpu.VMEM((PAGE_SIZE * kh, LANES), F32), pltpu.VMEM((1, LANES), F32),
                        pltpu.VMEM((1, LANES), F32), pltpu.VMEM((HEAD_W, LANES), F32)])
    return pl.pallas_call(
        functools.partial(_attn_sample_kernel, n_new=n_new, n_pp=n_pp, lam_init=lam_init),
        grid_spec=grid_spec,
        out_shape=jax.ShapeDtypeStruct((b, n_new, d), F32),
        compiler_params=_params(("arbitrary", "arbitrary")),
        name="attn_sample",
    )(page_table.reshape(-1), qrows, kn, vn, *([k_view] * n_pp), *([v_view] * n_pp), bias, bias_new, *lams, subln)


def _router_kernel(xp_ref, xs_ref, g_ref, wh_ref, wl_ref, b_ref, tri_ref, xn_ref, ids_ref, gates_ref, cnt_ref,
                   run_sc, *, n_p, n_rows):
    i = pl.program_id(0)
    tm = xp_ref.shape[0]

    @pl.when(i == 0)
    def _():
        run_sc[...] = jnp.zeros(run_sc.shape, F32)

    x = jnp.where(i < n_p, xp_ref[...], xs_ref[...])
    xn = _rms(x, g_ref[...], RMS_EPS)
    xn_ref[...] = xn
    xh, xl = _split3(xn)
    d = functools.partial(jnp.dot, preferred_element_type=F32)
    logits = d(xh, wh_ref[...]) + d(xl, wh_ref[...]) + d(xh, wl_ref[...]) + b_ref[...]
    lane = lax.broadcasted_iota(jnp.int32, logits.shape, 1)
    ninf = -jnp.inf
    gl = jnp.where(lane < N_EXPERT_GROUPS, logits, ninf)
    gmax = jnp.max(gl, axis=-1, keepdims=True)
    g_val = 1.0 / jnp.sum(jnp.exp(gl - gmax), axis=-1, keepdims=True)
    g_idx = jnp.min(jnp.where(gl == gmax, lane, LANES), axis=-1, keepdims=True)
    lo = N_EXPERT_GROUPS + EXPERTS_PER_GROUP * g_idx
    el = jnp.where((lane >= lo) & (lane < lo + EXPERTS_PER_GROUP), logits, ninf)
    e1 = jnp.max(el, axis=-1, keepdims=True)
    i1 = jnp.min(jnp.where(el == e1, lane, LANES), axis=-1, keepdims=True)
    el2 = jnp.where(lane == i1, ninf, el)
    e2 = jnp.max(el2, axis=-1, keepdims=True)
    i2 = jnp.min(jnp.where(el2 == e2, lane, LANES), axis=-1, keepdims=True)
    r = jnp.exp(e2 - e1)
    w1 = g_val / (1.0 + r)
    w2 = g_val * r / (1.0 + r)
    chose = jnp.where(lane == i1, 1.0, jnp.where(lane == i2, 1.0, 0.0))
    real = (i * tm + lax.broadcasted_iota(jnp.int32, logits.shape, 0)) < n_rows
    chose = jnp.where(real, chose, 0.0)
    before = jnp.dot(tri_ref[...], chose.astype(BF16), preferred_element_type=F32) + run_sc[...]
    r1 = jnp.sum(jnp.where(lane == i1, before, 0.0), axis=-1, keepdims=True).astype(jnp.int32)
    r2 = jnp.sum(jnp.where(lane == i2, before, 0.0), axis=-1, keepdims=True).astype(jnp.int32)
    run_sc[...] = run_sc[...] + jnp.sum(chose, axis=0, keepdims=True)
    ids_ref[...] = jnp.where(lane == 0, i1 - N_EXPERT_GROUPS, jnp.where(lane == 1, i2 - N_EXPERT_GROUPS,
                             jnp.where(lane == 2, r1, jnp.where(lane == 3, r2, 0))))
    gates_ref[...] = jnp.where(lane == 0, w1, jnp.where(lane == 1, w2, 0.0))

    @pl.when(i == pl.num_programs(0) - 1)
    def _():
        cnt_ref[...] = run_sc[...]


def moe_route(xp, xs, g, w_hi, w_lo, bias, tm):
    (tp, d), ts = xp.shape, xs.shape[0]
    n_p, n_s = tp // tm, -(-ts // tm)
    xs = jnp.pad(xs, ((0, n_s * tm - ts), (0, 0)))
    t = tp + n_s * tm
    row = lambda i: (i, 0)
    const = lambda i: (0, 0)
    tri = jnp.tril(jnp.ones((tm, tm), BF16), -1)
    return pl.pallas_call(
        functools.partial(_router_kernel, n_p=n_p, n_rows=tp + ts),
        grid=(n_p + n_s,),
        in_specs=[pl.BlockSpec((tm, d), lambda i: (jnp.minimum(i, n_p - 1), 0)),
                  pl.BlockSpec((tm, d), lambda i: (jnp.maximum(i - n_p, 0), 0)),
                  pl.BlockSpec((1, d), const),
                  pl.BlockSpec((d, LANES), const), pl.BlockSpec((d, LANES), const),
                  pl.BlockSpec((1, LANES), const), pl.BlockSpec((tm, tm), const)],
        out_specs=[pl.BlockSpec((tm, d), row), pl.BlockSpec((tm, LANES), row), pl.BlockSpec((tm, LANES), row),
                   pl.BlockSpec((1, LANES), const)],
        out_shape=[jax.ShapeDtypeStruct((t, d), F32), jax.ShapeDtypeStruct((t, LANES), jnp.int32),
                   jax.ShapeDtypeStruct((t, LANES), F32), jax.ShapeDtypeStruct((1, LANES), F32)],
        scratch_shapes=[pltpu.VMEM((1, LANES), F32)],
        compiler_params=_params(("arbitrary",)),
        name="moe_route",
    )(xp, xs, g, w_hi, w_lo, bias, tri)


SUBLANES = 8
GATHER_UNROLL = 32


def _row_gather_start(idx_ref, base, n, src_hbm, dst, sem):
    tiles = GATHER_UNROLL // SUBLANES

    def body(g, c):
        for j in range(GATHER_UNROLL):
            idx = idx_ref[base + g * GATHER_UNROLL + j]
            pltpu.make_async_copy(src_hbm.at[pl.ds(idx, 1), :],
                                  dst.at[g * tiles + j // SUBLANES, pl.ds(j % SUBLANES, 1), :], sem).start()
        return c
    lax.fori_loop(0, n // GATHER_UNROLL, body, 0)


def _row_gather_wait(dst, sem):
    pltpu.make_async_copy(dst, dst, sem).wait()


W_CHUNK = 256
W_STAGES = 4


def _expert_kernel(bexp_ref, rtok_ref, nused_ref, nxt_ref, c0_ref, c1_ref, slot_ref,
                   xn_hbm, w13_hbm, w2_hbm, ys_ref, xbuf, sem, wb13, wb2, stage, wsem, *, tm, layer):
    rb = pl.program_id(0)
    nused = nused_ref[0]
    n13 = wb13.shape[1] // W_CHUNK
    nc = n13 + wb2.shape[1] // W_CHUNK

    def start_chunk(e, c):
        s = c % W_STAGES

        @pl.when(c < n13)
        def _():
            pltpu.make_async_copy(w13_hbm.at[layer, e, pl.ds(c * W_CHUNK, W_CHUNK), :], stage.at[s], wsem.at[s]).start()

        @pl.when(c >= n13)
        def _():
            pltpu.make_async_copy(w2_hbm.at[layer, e, pl.ds((c - n13) * W_CHUNK, W_CHUNK), :], stage.at[s],
                                  wsem.at[s]).start()

    def finish_chunk(c, dst):
        s = c % W_STAGES
        pltpu.make_async_copy(stage.at[s], stage.at[s], wsem.at[s]).wait()
        v = stage[s].astype(BF16)

        @pl.when(c < n13)
        def _():
            wb13[dst, pl.ds(pl.multiple_of(c * W_CHUNK, W_CHUNK), W_CHUNK), :] = v

        @pl.when(c >= n13)
        def _():
            wb2[dst, pl.ds(pl.multiple_of((c - n13) * W_CHUNK, W_CHUNK), W_CHUNK), :] = v

    def stream(e, lo, hi, dst):
        @pl.when((lo == 0) & (hi > 0))
        def _():
            for c in range(W_STAGES):
                start_chunk(e, jnp.int32(c))

        def body(c, carry):
            finish_chunk(c, dst)

            @pl.when(c + W_STAGES < nc)
            def _():
                start_chunk(e, c + W_STAGES)
            return carry

        lax.fori_loop(lo, hi, body, 0)

    @pl.when(rb == 0)
    def _():
        stream(bexp_ref[0], 0, nc, 0)
        _row_gather_start(rtok_ref, 0, tm, xn_hbm, xbuf.at[0], sem.at[0])

    @pl.when(rb + 1 < nused)
    def _():
        nxt = (rb + 1) % 2
        _row_gather_start(rtok_ref, (rb + 1) * tm, tm, xn_hbm, xbuf.at[nxt], sem.at[nxt])

    @pl.when(rb < nused)
    def _():
        cur = slot_ref[rb]
        stream(nxt_ref[rb], c0_ref[rb], c1_ref[rb], 1 - cur)
        slot = rb % 2
        _row_gather_wait(xbuf.at[slot], sem.at[slot])
        x = xbuf[slot].reshape(tm, xbuf.shape[-1]).astype(BF16)
        h = jnp.dot(x, wb13[cur], preferred_element_type=F32)
        ga, up = h[:, :EXPERT_FF], h[:, EXPERT_FF:]
        a = ga * (1.0 / (1.0 + jnp.exp(-ga))) * up
        ys_ref[...] = jnp.dot(a.astype(BF16), wb2[cur], preferred_element_type=F32)

    @pl.when(rb >= nused)
    def _():
        ys_ref[...] = jnp.zeros(ys_ref.shape, F32)


def moe_experts(xn, w13, w2, layer, block_exp, row_tok, n_used, sched, tm):
    t, d = xn.shape
    n_blocks = block_exp.shape[0]
    anyspace = pl.BlockSpec(memory_space=pl.ANY)
    grid_spec = pltpu.PrefetchScalarGridSpec(
        num_scalar_prefetch=7,
        grid=(n_blocks,),
        in_specs=[anyspace, anyspace, anyspace],
        out_specs=pl.BlockSpec((tm, d), lambda rb, *_: (rb, 0)),
        scratch_shapes=[pltpu.VMEM((2, tm // SUBLANES, SUBLANES, d), F32), pltpu.SemaphoreType.DMA((2,)),
                        pltpu.VMEM((2, d, 2 * EXPERT_FF), BF16), pltpu.VMEM((2, EXPERT_FF, d), BF16),
                        pltpu.VMEM((W_STAGES, W_CHUNK, d), F32), pltpu.SemaphoreType.DMA((W_STAGES,))])
    return pl.pallas_call(
        functools.partial(_expert_kernel, tm=tm, layer=layer),
        grid_spec=grid_spec,
        out_shape=jax.ShapeDtypeStruct((n_blocks * tm, d), F32),
        compiler_params=_params(("arbitrary",)),
        name="moe_experts",
    )(block_exp, row_tok, n_used, *sched, xn, w13, w2)


def _weight_schedule(padded, block_exp, tm, n_chunks):
    e_idx = jnp.arange(N_EXPERTS, dtype=jnp.int32)
    nonempty = padded > 0
    run = jnp.cumsum(nonempty.astype(jnp.int32)) - 1
    later = jnp.where(nonempty, e_idx, N_EXPERTS)
    nxt = jnp.concatenate([lax.cummin(later[::-1])[::-1][1:], jnp.full((1,), N_EXPERTS, jnp.int32)])
    has_next = nxt < N_EXPERTS
    nblk = padded // tm
    first = (jnp.cumsum(padded) - padded) // tm
    e = block_exp
    j = jnp.arange(block_exp.shape[0], dtype=jnp.int32) - first[e]
    n = jnp.maximum(nblk[e], 1)
    live = has_next[e] & (j >= 0) & (j < nblk[e])
    lo = jnp.where(live, j * n_chunks // n, 0)
    hi = jnp.where(live, (j + 1) * n_chunks // n, 0)
    return (jnp.where(has_next[e], nxt[e], e).astype(jnp.int32), lo.astype(jnp.int32), hi.astype(jnp.int32),
            (run[e] % 2).astype(jnp.int32))


def _combine_kernel(pos_ref, ys_hbm, x_ref, gates_ref, gout_ref, o_ref, ybuf, sem, *, tm, norm_out):
    i = pl.program_id(0)
    n = pl.num_programs(0)

    def start(tile, slot):
        _row_gather_start(pos_ref, tile * 2 * tm, 2 * tm, ys_hbm, ybuf.at[slot], sem.at[slot])

    @pl.when(i == 0)
    def _():
        start(0, 0)

    @pl.when(i + 1 < n)
    def _():
        start(i + 1, (i + 1) % 2)

    slot = i % 2
    _row_gather_wait(ybuf.at[slot], sem.at[slot])
    g = gates_ref[...]
    nt, d = tm // SUBLANES, ybuf.shape[-1]
    y0 = ybuf[slot, :nt].reshape(tm, d)
    y1 = ybuf[slot, nt:].reshape(tm, d)
    out = x_ref[...] + (g[:, 0:1] * y0 + g[:, 1:2] * y1)
    o_ref[...] = _rms(out, gout_ref[...], RMS_EPS) if norm_out else out


def moe_combine(ys, x, gates, pos, tm, g_out, norm_out):
    t, d = x.shape
    pos_tiles = pos.reshape(t // tm, tm, 2).transpose(0, 2, 1).reshape(-1)
    grid_spec = pltpu.PrefetchScalarGridSpec(
        num_scalar_prefetch=1,
        grid=(t // tm,),
        in_specs=[pl.BlockSpec(memory_space=pl.ANY),
                  pl.BlockSpec((tm, d), lambda i, ps: (i, 0)),
                  pl.BlockSpec((tm, LANES), lambda i, ps: (i, 0)),
                  pl.BlockSpec((1, d), lambda i, ps: (0, 0))],
        out_specs=pl.BlockSpec((tm, d), lambda i, ps: (i, 0)),
        scratch_shapes=[pltpu.VMEM((2, 2 * tm // SUBLANES, SUBLANES, d), F32), pltpu.SemaphoreType.DMA((2,))])
    return pl.pallas_call(
        functools.partial(_combine_kernel, tm=tm, norm_out=norm_out),
        grid_spec=grid_spec,
        out_shape=jax.ShapeDtypeStruct((t, d), F32),
        compiler_params=_params(("arbitrary",)),
        name="moe_combine",
    )(pos_tiles, ys, x, gates, g_out)


def hier_moe(xp, xs, g, w_group, b_group, w_expert, b_expert, w13, w2, layer, g_out, norm_out,
             tm_blk=256, tm_tok=128, tm_route=512):
    tp, d = xp.shape
    pad = LANES - N_EXPERT_GROUPS - N_EXPERTS
    w_cat = jnp.concatenate([w_group, w_expert, jnp.zeros((d, pad), F32)], axis=1)
    b_cat = jnp.concatenate([b_group, b_expert, jnp.zeros((pad,), F32)])[None]
    w_hi = w_cat.astype(BF16)
    w_lo = (w_cat - w_hi.astype(F32)).astype(BF16)
    xn, ids, gates, cnt = moe_route(xp, xs, g, w_hi, w_lo, b_cat, tm_route)
    t = tp + xs.shape[0]
    ids = ids[:t]
    n_assign = 2 * t
    counts = cnt[0, N_EXPERT_GROUPS:N_EXPERT_GROUPS + N_EXPERTS].astype(jnp.int32)
    padded = ((counts + tm_blk - 1) // tm_blk) * tm_blk
    pad_end = jnp.cumsum(padded)
    pad_start = pad_end - padded
    pos = pad_start[ids[:, :2]] + ids[:, 2:4]
    n_blocks = -(-n_assign // tm_blk) + N_EXPERTS
    flat_t = jnp.arange(n_assign, dtype=jnp.int32) // 2
    row_tok = jnp.zeros((n_blocks * tm_blk,), jnp.int32).at[pos.reshape(n_assign)].set(flat_t)
    blk_start = jnp.arange(n_blocks, dtype=jnp.int32) * tm_blk
    block_exp = jnp.minimum(jnp.sum((pad_end[None, :] <= blk_start[:, None]).astype(jnp.int32), axis=1), N_EXPERTS - 1)
    n_used = (pad_end[-1:] // tm_blk).astype(jnp.int32)
    n_wchunks = (w13.shape[2] + w2.shape[2]) // W_CHUNK
    sched = _weight_schedule(padded, block_exp, tm_blk, n_wchunks)
    ys = moe_experts(xn, w13, w2, layer, block_exp, row_tok, n_used, sched, tm_blk)
    return (moe_combine(ys, xp, gates[:tp], pos[:tp], tm_tok, g_out, norm_out),
            moe_combine(ys, xs, gates[tp:t], pos[tp:], min(tm_tok, xs.shape[0]), g_out, norm_out))


def _cmul(ar, ai, br, bi):
    return ar * br - ai * bi, ar * bi + ai * br


def _ssm_prep_kernel(are_ref, aim_ref, ldt_ref, bre_ref, bim_ref, cre_ref, cim_ref,
                     m_ref, bc_ref, cc_ref, a16_ref, a4_ref, *, n_new):
    p = SSM_STATE
    nc = SSM_CHUNK
    a_re, a_im = are_ref[...], aim_ref[...]
    dt = jnp.exp(ldt_ref[...])
    mag = jnp.exp(dt * a_re)
    ab_re = mag * jnp.cos(dt * a_im)
    ab_im = mag * jnp.sin(dt * a_im)
    den = a_re * a_re + a_im * a_im
    nr = ab_re - 1.0
    fr = (nr * a_re + ab_im * a_im) / den
    fi = (ab_im * a_re - nr * a_im) / den
    bt_re, bt_im = bre_ref[...], bim_ref[...]
    bb_re = fr * bt_re - fi * bt_im
    bb_im = fr * bt_im + fi * bt_re
    c_re, c_im = cre_ref[...], cim_ref[...]
    pw = [(jnp.ones_like(ab_re), jnp.zeros_like(ab_re))]
    for _ in range(nc):
        pw.append(_cmul(pw[-1][0], pw[-1][1], ab_re, ab_im))
    wst = []
    for tau in range(nc + 1):
        wr, wi = _cmul(c_re, c_im, pw[tau][0], pw[tau][1])
        wst.append(jnp.concatenate([wr, -wi], axis=1))
    wstack = jnp.concatenate(wst[:nc], axis=0)
    bs_t = jnp.concatenate([bb_re, bb_im], axis=1)
    q = _dot3(bs_t, wstack, _NT)
    qpad = jnp.concatenate([jnp.zeros_like(q), q], axis=1)
    w = nc * SSM_GROUP_CH
    rows = []
    for s in range(nc):
        off = (nc - s) * SSM_GROUP_CH
        rows.append(qpad[:, off:off + w])
    m_ref[...] = jnp.concatenate(rows, axis=0).astype(m_ref.dtype)
    bc = []
    for s in range(nc):
        xr, xi = _cmul(bb_re, bb_im, pw[nc - 1 - s][0], pw[nc - 1 - s][1])
        bc.append(jnp.concatenate([xr, xi], axis=1))
    bc_ref[...] = jnp.concatenate(bc, axis=0).astype(bc_ref.dtype)
    cc_ref[...] = jnp.concatenate(wst[1:], axis=0).astype(cc_ref.dtype)
    a16_ref[...] = jnp.concatenate([pw[nc][0], pw[nc][1]], axis=1)
    a4_ref[...] = jnp.concatenate([pw[n_new][0], pw[n_new][1]], axis=1)


def ssm_prep(a_re, a_im, log_dt, b_re, b_im, c_re, c_im, n_new):
    g, p = a_re.shape
    gc = SSM_GROUP_CH
    w = SSM_CHUNK * gc
    vec = pl.BlockSpec((None, 1, p), lambda i: (i, 0, 0))
    mat = pl.BlockSpec((None, gc, p), lambda i: (i, 0, 0))
    bt_re = b_re.transpose(0, 2, 1)
    bt_im = b_im.transpose(0, 2, 1)
    return pl.pallas_call(
        functools.partial(_ssm_prep_kernel, n_new=n_new),
        grid=(g,),
        in_specs=[vec, vec, pl.BlockSpec((None, 1, 1), lambda i: (i, 0, 0)), mat, mat, mat, mat],
        out_specs=[pl.BlockSpec((None, w, w), lambda i: (i, 0, 0)),
                   pl.BlockSpec((None, w, 2 * p), lambda i: (i, 0, 0)),
                   pl.BlockSpec((None, w, 2 * p), lambda i: (i, 0, 0)),
                   pl.BlockSpec((None, 1, 2 * p), lambda i: (i, 0, 0)),
                   pl.BlockSpec((None, 1, 2 * p), lambda i: (i, 0, 0))],
        out_shape=[jax.ShapeDtypeStruct((g, w, w), BF16),
                   jax.ShapeDtypeStruct((g, w, 2 * p), BF16),
                   jax.ShapeDtypeStruct((g, w, 2 * p), BF16),
                   jax.ShapeDtypeStruct((g, 1, 2 * p), F32),
                   jax.ShapeDtypeStruct((g, 1, 2 * p), F32)],
        compiler_params=_params(("arbitrary",)),
        name="ssm_prep",
    )(a_re[:, None], a_im[:, None], log_dt[:, None, None], bt_re, bt_im, c_re, c_im)


GROUPS_PER_VREG = LANES // SSM_GROUP_CH
PERM_BATCH = 8


def _chunk_perm():
    n = GROUPS_PER_VREG
    idx = np.arange(n * LANES)
    a, b, c = idx // LANES, (idx // SSM_GROUP_CH) % n, idx % SSM_GROUP_CH
    m = np.zeros((n * LANES, n * LANES), np.float32)
    m[idx, (b * n + a) * SSM_GROUP_CH + c] = 1.0
    return jnp.asarray(m, BF16)


def _ssm_in_kernel(*refs):
    nj = (len(refs) - 3) // 2
    x_refs, g_refs = refs[:nj], refs[nj:2 * nj]
    p_ref, u_ref, inv_sc = refs[2 * nj], refs[2 * nj + 1], refs[2 * nj + 2]
    tm = x_refs[0].shape[0]
    nc, n = tm // SSM_CHUNK, GROUPS_PER_VREG
    ssq = jnp.zeros((tm, LANES), F32)
    for x_ref in x_refs:
        xj = x_ref[...]
        ssq = ssq + xj * xj
    inv = lax.rsqrt(jnp.sum(ssq, axis=-1, keepdims=True) * (1.0 / (nj * LANES)) + RMS_EPS)
    inv_sc[...] = jnp.broadcast_to(inv, (tm, LANES))

    def piece(j, s):
        rows = pl.ds(s, nc, stride=SSM_CHUNK)
        return x_refs[j][rows, :] * inv_sc[rows, :] * g_refs[j][...]

    pairs = [(j, h) for j in range(nj) for h in range(SSM_CHUNK // n)]
    for p0 in range(0, len(pairs), PERM_BATCH):
        batch = pairs[p0:p0 + PERM_BATCH]
        lhs = jnp.concatenate(
            [jnp.concatenate([piece(j, h * n + sl) for sl in range(n)], axis=1).astype(BF16) for j, h in batch],
            axis=0)
        r = jnp.dot(lhs, p_ref[...], preferred_element_type=F32)
        for k, (j, h) in enumerate(batch):
            for gl in range(n):
                u_ref[j * n + gl, :, h * LANES:(h + 1) * LANES] = (
                    r[k * nc:(k + 1) * nc, gl * LANES:(gl + 1) * LANES].astype(BF16))


def ssm_chunk_in(x, g, perm, tm):
    t, d = x.shape
    ng, w = d // SSM_GROUP_CH, SSM_CHUNK * SSM_GROUP_CH
    nc, nj = tm // SSM_CHUNK, d // LANES
    return pl.pallas_call(
        _ssm_in_kernel,
        grid=(t // tm,),
        in_specs=[pl.BlockSpec((tm, LANES), lambda i, j=j: (i, j)) for j in range(nj)]
                 + [pl.BlockSpec((1, LANES), lambda i, j=j: (0, j)) for j in range(nj)]
                 + [pl.BlockSpec(perm.shape, lambda i: (0, 0))],
        out_specs=pl.BlockSpec((ng, nc, w), lambda i: (0, i, 0)),
        out_shape=jax.ShapeDtypeStruct((ng, t // SSM_CHUNK, w), BF16),
        scratch_shapes=[pltpu.VMEM((tm, LANES), F32)],
        compiler_params=_params(("arbitrary",)),
        name="ssm_chunk_in",
    )(*([x] * nj), *([g] * nj), perm)


def _ssm_out_kernel(z_ref, p_ref, o_ref):
    nj, tm, _ = o_ref.shape
    nc, n = tm // SSM_CHUNK, GROUPS_PER_VREG
    pairs = [(j, h) for j in range(nj) for h in range(SSM_CHUNK // n)]
    for p0 in range(0, len(pairs), PERM_BATCH):
        batch = pairs[p0:p0 + PERM_BATCH]
        lhs = jnp.concatenate(
            [jnp.concatenate([z_ref[j * n + gl, :, h * LANES:(h + 1) * LANES] for gl in range(n)], axis=1)
             for j, h in batch], axis=0)
        r = jnp.dot(lhs, p_ref[...], preferred_element_type=F32)
        for k, (j, h) in enumerate(batch):
            for tl in range(n):
                o_ref[j, pl.ds(h * n + tl, nc, stride=SSM_CHUNK), :] = (
                    r[k * nc:(k + 1) * nc, tl * LANES:(tl + 1) * LANES])


def ssm_chunk_out(z, perm, tm, d):
    ng, n_c, w = z.shape
    nc, nj = tm // SSM_CHUNK, d // LANES
    return pl.pallas_call(
        _ssm_out_kernel,
        grid=(n_c // nc,),
        in_specs=[pl.BlockSpec((ng, nc, w), lambda i: (0, i, 0)), pl.BlockSpec(perm.shape, lambda i: (0, 0))],
        out_specs=pl.BlockSpec((nj, tm, LANES), lambda i: (0, i, 0)),
        out_shape=jax.ShapeDtypeStruct((nj, n_c * SSM_CHUNK, LANES), F32),
        compiler_params=_params(("arbitrary",)),
        name="ssm_chunk_out",
    )(z, perm)


def _ssm_x_kernel(u_ref, bc_ref, x_ref, *, k):
    for gi in range(u_ref.shape[0]):
        bc = bc_ref[gi]
        x_ref[gi] = jnp.dot(u_ref[gi, :, :k], bc[bc.shape[0] - k:], preferred_element_type=F32)


def _ssm_scan_kernel(xp_ref, xs_ref, h0_ref, a16_ref, a4_ref, hs_ref, hfin_ref, hsfin_ref, *, n_chunks, batch):
    p = SSM_STATE

    def coef(a):
        ar, ai = a[:, :, :p], a[:, :, p:]
        return jnp.concatenate([ar, ar], axis=-1), jnp.concatenate([-ai, ai], axis=-1)

    def step(h, c1, c2):
        return c1 * h + c2 * pltpu.roll(h, p, 2)

    c1, c2 = coef(a16_ref[...])

    def body(c, h):
        rows = pl.ds(c, batch, stride=n_chunks)
        hs_ref[:, rows, :] = h
        return step(h, c1, c2) + xp_ref[:, rows, :]

    h0 = jnp.zeros((xp_ref.shape[0], batch, 2 * p), F32)
    hfin_ref[...] = lax.fori_loop(0, n_chunks, body, h0)
    d1, d2 = coef(a4_ref[...])
    hsfin_ref[...] = step(h0_ref[...], d1, d2) + xs_ref[...]


def _ssm_y_kernel(u_ref, hs_ref, m_ref, cc_ref, d_ref, z_ref):
    c = math.sqrt(2.0 / math.pi)
    for gi in range(u_ref.shape[0]):
        u = u_ref[gi]
        y = jnp.dot(u, m_ref[gi], preferred_element_type=F32)
        y = y + lax.dot_general(hs_ref[gi].astype(BF16), cc_ref[gi], (_NT, ((), ())), preferred_element_type=F32)
        y = y + d_ref[gi] * u.astype(F32)
        z = 0.5 * y * (1.0 + jnp.tanh(c * (y + 0.044715 * (y * y * y))))
        z_ref[gi] = z.astype(z_ref.dtype)


def ssm_mix(up, us, h0_g, ops, d_tile, n_chunks, batch, n_new, g_blk=8, g_scan=16):
    m_g, bc_g, cc_g, a16, a4 = ops
    g, _, w = up.shape
    p2 = 2 * SSM_STATE
    n_s = us.shape[1]

    def blk(gb, *shape):
        return pl.BlockSpec((gb,) + shape, lambda i: (i,) + (0,) * len(shape))

    def chunk_state(u, k):
        r = u.shape[1]
        return pl.pallas_call(
            functools.partial(_ssm_x_kernel, k=k),
            grid=(g // g_blk,),
            in_specs=[blk(g_blk, r, w), blk(g_blk, w, p2)],
            out_specs=blk(g_blk, r, p2),
            out_shape=jax.ShapeDtypeStruct((g, r, p2), F32),
            compiler_params=_params(("arbitrary",)),
            name="ssm_chunk_state",
        )(u, bc_g)

    def output(u, hs):
        r = u.shape[1]
        return pl.pallas_call(
            _ssm_y_kernel,
            grid=(g // g_blk,),
            in_specs=[blk(g_blk, r, w), blk(g_blk, r, p2), blk(g_blk, w, w), blk(g_blk, w, p2), blk(g_blk, 1, w)],
            out_specs=blk(g_blk, r, w),
            out_shape=jax.ShapeDtypeStruct((g, r, w), BF16),
            compiler_params=_params(("arbitrary",)),
            name="ssm_output",
        )(u, hs, m_g, cc_g, d_tile)

    xp = chunk_state(up, w)
    xs = chunk_state(us, n_new * SSM_GROUP_CH)
    r = up.shape[1]
    hs, hfin, hsfin = pl.pallas_call(
        functools.partial(_ssm_scan_kernel, n_chunks=n_chunks, batch=batch),
        grid=(g // g_scan,),
        in_specs=[blk(g_scan, r, p2), blk(g_scan, n_s, p2), blk(g_scan, n_s, p2), blk(g_scan, 1, p2),
                  blk(g_scan, 1, p2)],
        out_specs=[blk(g_scan, r, p2), blk(g_scan, batch, p2), blk(g_scan, n_s, p2)],
        out_shape=[jax.ShapeDtypeStruct((g, r, p2), F32),
                   jax.ShapeDtypeStruct((g, batch, p2), F32),
                   jax.ShapeDtypeStruct((g, n_s, p2), F32)],
        compiler_params=_params(("arbitrary",)),
        name="ssm_scan",
    )(xp, xs, h0_g, a16, a4)
    return output(up, hs), output(us, h0_g), hfin, hsfin


def kernel(x_prompt, x_sample, cache_k, cache_v, state_ssm_re, state_ssm_im, page_table, norm_mix, norm_ffn, norm_final, attn_w_qkv, attn_lambda_q1, attn_lambda_k1, attn_lambda_q2, attn_lambda_k2, attn_subln, attn_w_o, ssm_A_re, ssm_A_im, ssm_log_dt, ssm_B_re, ssm_B_im, ssm_C_re, ssm_C_im, ssm_D, ssm_glu_w, ssm_glu_b, moe_w_group, moe_b_group, moe_w_expert, moe_b_expert, moe_w13, moe_w2):
    batch, seq, d = x_prompt.shape
    dec_b, dec_s, _ = x_sample.shape
    depth = norm_mix.shape[0]
    tp, ts = batch * seq, dec_b * dec_s
    xp = x_prompt.reshape(tp, d)
    xs = x_sample.reshape(ts, d)
    g, pst, gc = ssm_B_re.shape[1:]
    n_chunks = seq // SSM_CHUNK
    k_p, v_p, k_s, v_s = [], [], [], []
    hr_p, hi_p, hr_s, hi_s = [], [], [], []
    for i in range(depth):
        gm = norm_mix[i][None]
        if i % 2 == 0:
            a = i // 2
            lam_init = 0.8 - 0.6 * math.exp(-0.3 * i)
            lams = (attn_lambda_q1[a][None], attn_lambda_k1[a][None], attn_lambda_q2[a][None], attn_lambda_k2[a][None])
            subln = attn_subln[a][None]
            w_qkv = attn_w_qkv[a].astype(BF16)
            w_o = attn_w_o[a].astype(BF16)
            qp, khp, kp, vp = qkv_proj(xp, gm, w_qkv, 1024, 512)
            qs, _, ks, vs = qkv_proj(xs, gm, w_qkv, ts, 512)
            op = attn_prompt(qp, khp, vp, lams, subln, batch, seq, lam_init)
            os_ = attn_sample(qs.reshape(dec_b, dec_s, d), ks.reshape(dec_b, dec_s * 2 * N_HEADS, HEAD_DIM),
                              vs.reshape(dec_b, dec_s, d), cache_k, cache_v, page_table, lams, subln, a, lam_init)
            xp = matmul_residual(op, w_o, xp, 1024, 512)
            xs = matmul_residual(os_.reshape(ts, d), w_o, xs, ts, 512)
            k_p.append(kp.reshape(batch, seq, N_HEADS, 2, HEAD_DIM))
            v_p.append(vp.reshape(batch, seq, N_HEADS, HEAD_W))
            k_s.append(ks.reshape(dec_b, dec_s, N_HEADS, 2, HEAD_DIM))
            v_s.append(vs.reshape(dec_b, dec_s, N_HEADS, HEAD_W))
        else:
            s = i // 2
            ops = ssm_prep(ssm_A_re[s], ssm_A_im[s], ssm_log_dt[s], ssm_B_re[s], ssm_B_im[s],
                           ssm_C_re[s], ssm_C_im[s], dec_s)
            perm = _chunk_perm()
            up = ssm_chunk_in(xp, gm, perm, 1024)
            hs = rms_norm(xs, gm, ts)
            us = hs.reshape(dec_b, dec_s, g, gc).transpose(2, 0, 1, 3).reshape(g, dec_b, dec_s * gc)
            us = jnp.pad(us, ((0, 0), (0, 0), (0, (SSM_CHUNK - dec_s) * gc))).astype(BF16)
            h0_g = jnp.concatenate([state_ssm_re[s], state_ssm_im[s]], axis=-1).transpose(1, 0, 2)
            d_tile = jnp.tile(ssm_D[s], (1, SSM_CHUNK))[:, None]
            zp_g, zs_g, hfin, hsfin = ssm_mix(up, us, h0_g, ops, d_tile, n_chunks, batch, dec_s)
            zp = ssm_chunk_out(zp_g, perm, 1024, d)
            zs = zs_g[:, :, :dec_s * gc].reshape(g, dec_b, dec_s, gc).transpose(1, 2, 0, 3).reshape(ts, d)
            zs = zs.astype(F32).reshape(ts, d // LANES, LANES).transpose(1, 0, 2)
            glu_w = ssm_glu_w[s].astype(BF16)
            glu_b = ssm_glu_b[s][None]
            xp = glu_residual(zp, glu_w, glu_b, xp, 1024, 512)
            xs = glu_residual(zs, glu_w, glu_b, xs, ts, 512)
            hfin = hfin.transpose(1, 0, 2)
            hsfin = hsfin.transpose(1, 0, 2)
            hr_p.append(hfin[..., :pst])
            hi_p.append(hfin[..., pst:])
            hr_s.append(hsfin[..., :pst])
            hi_s.append(hsfin[..., pst:])
        xp, xs = hier_moe(xp, xs, norm_ffn[i][None], moe_w_group[i], moe_b_group[i], moe_w_expert[i],
                          moe_b_expert[i], moe_w13, moe_w2, i, norm_final[None], i == depth - 1)
    y_prompt = xp.reshape(batch, seq, d)
    y_sample = xs.reshape(dec_b, dec_s, d)
    return (y_prompt, y_sample, jnp.stack(k_p), jnp.stack(v_p), jnp.stack(k_s), jnp.stack(v_s),
            jnp.stack(hr_p), jnp.stack(hi_p), jnp.stack(hr_s), jnp.stack(hi_s))
```

```python
import functools
import math

import numpy as np
import jax
import jax.numpy as jnp
from jax import lax
from jax.experimental import pallas as pl
from jax.experimental.pallas import tpu as pltpu

F32 = jnp.float32
BF16 = jnp.bfloat16

RMS_EPS = 1e-6
SUBLN_EPS = 1e-5
HEAD_DIM = 128
HEAD_W = 2 * HEAD_DIM
N_HEADS = 8
Q_PRESCALE = HEAD_DIM ** -0.5 * math.log2(math.e)
PAGE_SIZE = 128
N_EXPERT_GROUPS = 4
EXPERTS_PER_GROUP = 8
N_EXPERTS = N_EXPERT_GROUPS * EXPERTS_PER_GROUP
EXPERT_FF = 1024
SSM_GROUP_CH = 16
SSM_STATE = 64
SSM_CHUNK = 16
LANES = 128
VMEM_LIMIT = 56 * 1024 * 1024


def _params(sem, vmem=VMEM_LIMIT):
    return pltpu.CompilerParams(dimension_semantics=sem, vmem_limit_bytes=vmem)


def _rms(x, g, eps):
    return x * lax.rsqrt(jnp.mean(x * x, axis=-1, keepdims=True) + eps) * g


def _split3(x):
    hi = x.astype(BF16)
    lo = (x - hi.astype(F32)).astype(BF16)
    return hi, lo


def _dot3(a, b, dims):
    ah, al = _split3(a)
    bh, bl = _split3(b)
    dn = (dims, ((), ()))
    d = functools.partial(lax.dot_general, dimension_numbers=dn, preferred_element_type=F32)
    return d(ah, bh) + d(al, bh) + d(ah, bl)


_NT = ((1,), (1,))
_NN = ((1,), (0,))


def _qkv_kernel(x_ref, g_ref, w_ref, q_ref, kh_ref, kn_ref, v_ref, xn_ref, *, nj):
    j = pl.program_id(1)
    tn = w_ref.shape[1]

    @pl.when(j == 0)
    def _():
        xn_ref[...] = _rms(x_ref[...], g_ref[...], RMS_EPS).astype(BF16)

    acc = jnp.dot(xn_ref[...], w_ref[...], preferred_element_type=F32)

    @pl.when(j < nj)
    def _():
        q_ref[...] = (acc * Q_PRESCALE).astype(q_ref.dtype)

    for jj in range(nj):
        @pl.when(j == nj + jj)
        def _(jj=jj):
            kh_ref[...] = acc.astype(kh_ref.dtype)
            for i in range(tn // HEAD_DIM):
                kn_ref[:, jj * (tn // HEAD_DIM) + i, :] = acc[:, i * HEAD_DIM:(i + 1) * HEAD_DIM]

    @pl.when(j >= 2 * nj)
    def _():
        v_ref[...] = acc


def qkv_proj(x, g, w_bf, tm, tn):
    t, d = x.shape
    nj = d // tn

    def omap(part):
        return lambda i, j: (i, jnp.clip(j - part * nj, 0, nj - 1))

    return pl.pallas_call(
        functools.partial(_qkv_kernel, nj=nj),
        grid=(t // tm, 3 * nj),
        in_specs=[pl.BlockSpec((tm, d), lambda i, j: (i, 0)),
                  pl.BlockSpec((1, d), lambda i, j: (0, 0)),
                  pl.BlockSpec((d, tn), lambda i, j: (0, j))],
        out_specs=[pl.BlockSpec((tm, tn), omap(0)), pl.BlockSpec((tm, tn), omap(1)),
                   pl.BlockSpec((tm, d // HEAD_DIM, HEAD_DIM), lambda i, j: (i, 0, 0)),
                   pl.BlockSpec((tm, tn), omap(2))],
        out_shape=[jax.ShapeDtypeStruct((t, d), BF16), jax.ShapeDtypeStruct((t, d), BF16),
                   jax.ShapeDtypeStruct((t, d // HEAD_DIM, HEAD_DIM), F32), jax.ShapeDtypeStruct((t, d), F32)],
        scratch_shapes=[pltpu.VMEM((tm, d), BF16)],
        compiler_params=_params(("arbitrary", "arbitrary")),
        name="qkv_proj",
    )(x, g, w_bf)


def _mm_res_kernel(a_ref, w_ref, x_ref, o_ref):
    o_ref[...] = x_ref[...] + jnp.dot(a_ref[...].astype(BF16), w_ref[...], preferred_element_type=F32)


def matmul_residual(a, w_bf, x, tm, tn):
    t, kd = a.shape
    n = w_bf.shape[1]
    return pl.pallas_call(
        _mm_res_kernel,
        grid=(t // tm, n // tn),
        in_specs=[pl.BlockSpec((tm, kd), lambda i, j: (i, 0)),
                  pl.BlockSpec((kd, tn), lambda i, j: (0, j)),
                  pl.BlockSpec((tm, tn), lambda i, j: (i, j))],
        out_specs=pl.BlockSpec((tm, tn), lambda i, j: (i, j)),
        out_shape=jax.ShapeDtypeStruct((t, n), F32),
        compiler_params=_params(("arbitrary", "arbitrary")),
        name="matmul_residual",
    )(a, w_bf, x)


def _glu_kernel(z_ref, w1_ref, w2_ref, b1_ref, b2_ref, x_ref, o_ref, zb):
    @pl.when(pl.program_id(1) == 0)
    def _():
        for j in range(z_ref.shape[0]):
            zb[:, j * LANES:(j + 1) * LANES] = z_ref[j].astype(BF16)

    z = zb[...]
    a = jnp.dot(z, w1_ref[...], preferred_element_type=F32) + b1_ref[...]
    b = jnp.dot(z, w2_ref[...], preferred_element_type=F32) + b2_ref[...]
    o_ref[...] = x_ref[...] + a * (1.0 / (1.0 + jnp.exp(-b)))


def glu_residual(z, w_bf, bias, x, tm, tn):
    nb, t, _ = z.shape
    d = nb * LANES
    nj = d // tn
    return pl.pallas_call(
        _glu_kernel,
        grid=(t // tm, nj),
        in_specs=[pl.BlockSpec((nb, tm, LANES), lambda i, j: (0, i, 0)),
                  pl.BlockSpec((d, tn), lambda i, j: (0, j)),
                  pl.BlockSpec((d, tn), lambda i, j: (0, j + nj)),
                  pl.BlockSpec((1, tn), lambda i, j: (0, j)),
                  pl.BlockSpec((1, tn), lambda i, j: (0, j + nj)),
                  pl.BlockSpec((tm, tn), lambda i, j: (i, j))],
        out_specs=pl.BlockSpec((tm, tn), lambda i, j: (i, j)),
        out_shape=jax.ShapeDtypeStruct((t, d), F32),
        scratch_shapes=[pltpu.VMEM((tm, d), BF16)],
        compiler_params=_params(("arbitrary", "arbitrary")),
        name="glu_residual",
    )(z, w_bf, w_bf, bias, bias, x)


def _norm_kernel(x_ref, g_ref, o_ref):
    o_ref[...] = _rms(x_ref[...], g_ref[...], RMS_EPS).astype(o_ref.dtype)


def rms_norm(x, g, tm):
    t, d = x.shape
    return pl.pallas_call(
        _norm_kernel,
        grid=(t // tm,),
        in_specs=[pl.BlockSpec((tm, d), lambda i: (i, 0)),
                  pl.BlockSpec((1, d), lambda i: (0, 0))],
        out_specs=pl.BlockSpec((tm, d), lambda i: (i, 0)),
        out_shape=jax.ShapeDtypeStruct((t, d), F32),
        compiler_params=_params(("arbitrary",)),
        name="rms_norm",
    )(x, g)


def _diff_lambda(lq1, lk1, lq2, lk2, lam_init):
    return (jnp.exp(jnp.sum(lq1[...] * lk1[...], axis=-1, keepdims=True))
            - jnp.exp(jnp.sum(lq2[...] * lk2[...], axis=-1, keepdims=True)) + lam_init)


def _attn_prompt_kernel(q_ref, k_ref, v_ref, lq1, lk1, lq2, lk2, subln_ref, o_ref,
                        vtb, acc_sc, *, tq, lam_init):
    seq = q_ref.shape[0]
    for j in range(seq // tq):
        vtb[j] = v_ref[j * tq:(j + 1) * tq, :].T.astype(BF16)
    lam = _diff_lambda(lq1, lk1, lq2, lk2, lam_init)
    krow = lax.broadcasted_iota(jnp.int32, (tq, tq), 0)
    qcol = lax.broadcasted_iota(jnp.int32, (tq, tq), 1)
    causal = krow <= qcol

    def q_body(qi, carry):
        q0 = pl.multiple_of(qi * tq, tq)
        qt = q_ref[pl.ds(q0, tq), :]
        acc_sc[...] = jnp.zeros(acc_sc.shape, F32)

        def kv_step(kj, masked, st):
            k0 = pl.multiple_of(kj * tq, tq)
            kt = k_ref[pl.ds(k0, tq), :]
            vt = vtb[kj]
            out = []
            for s in range(2):
                ls = slice(s * HEAD_DIM, (s + 1) * HEAD_DIM)
                sc = lax.dot_general(kt[:, ls], qt[:, ls], (_NT, ((), ())), preferred_element_type=F32)
                if masked:
                    sc = jnp.where(causal, sc, -jnp.inf)
                m_prev, l_prev = st[2 * s], st[2 * s + 1]
                m_new = jnp.maximum(m_prev, jnp.max(sc, axis=0, keepdims=True))
                alpha = jnp.exp2(m_prev - m_new)
                p = jnp.exp2(sc - m_new)
                out += [m_new, alpha * l_prev + jnp.sum(p, axis=0, keepdims=True)]
                acc_sc[s] = alpha * acc_sc[s] + jnp.dot(vt, p.astype(BF16), preferred_element_type=F32)
            return tuple(out)

        ninf = jnp.full((1, tq), -jnp.inf, F32)
        zero = jnp.zeros((1, tq), F32)
        st = lax.fori_loop(0, qi, lambda kj, st: kv_step(kj, False, st), (ninf, zero, ninf, zero))
        _, l0, _, l1 = kv_step(qi, True, st)
        ot = acc_sc[0] / l0 - lam * (acc_sc[1] / l1)
        ot = ot * lax.rsqrt(jnp.mean(ot * ot, axis=0, keepdims=True) + SUBLN_EPS)
        o_ref[pl.ds(q0, tq), :] = (ot.T * subln_ref[...] * (1.0 - lam_init)).astype(o_ref.dtype)
        return carry

    lax.fori_loop(0, seq // tq, q_body, 0)


def attn_prompt(q, k, v, lams, subln, batch, seq, lam_init, tq=512):
    t, d = q.shape
    blk = pl.BlockSpec((seq, HEAD_W), lambda b, h: (b, h))
    vec = pl.BlockSpec((1, HEAD_DIM), lambda b, h: (0, 0))
    return pl.pallas_call(
        functools.partial(_attn_prompt_kernel, tq=tq, lam_init=lam_init),
        grid=(batch, N_HEADS),
        in_specs=[blk, blk, blk, vec, vec, vec, vec,
                  pl.BlockSpec((1, HEAD_W), lambda b, h: (0, 0))],
        out_specs=blk,
        out_shape=jax.ShapeDtypeStruct((t, d), BF16),
        scratch_shapes=[pltpu.VMEM((seq // tq, HEAD_W, tq), BF16), pltpu.VMEM((2, HEAD_W, tq), F32)],
        compiler_params=_params(("arbitrary", "arbitrary")),
        name="attn_prompt",
    )(q, k, v, *lams, subln)


def _attn_sample_kernel(pt_ref, q_ref, kn_ref, vn_ref, *rest, n_new, n_pp, lam_init):
    del pt_ref
    kp_refs, vp_refs = rest[:n_pp], rest[n_pp:2 * n_pp]
    bias_ref, biasn_ref, lq1, lk1, lq2, lk2, subln_ref, o_ref, pbuf, m_sc, l_sc, acc_sc = rest[2 * n_pp:]
    p = pl.program_id(1)
    qrows = q_ref[...].astype(BF16)

    @pl.when(p == 0)
    def _():
        m_sc[...] = jnp.full(m_sc.shape, -jnp.inf, F32)
        l_sc[...] = jnp.zeros(l_sc.shape, F32)
        acc_sc[...] = jnp.zeros(acc_sc.shape, F32)

    def attend(k_rows, v_rows, bias):
        r = k_rows.shape[0]
        st = lax.dot_general(k_rows.astype(BF16), qrows, (_NT, ((), ())), preferred_element_type=F32)
        st = st.reshape(r // bias.shape[0], bias.shape[0], LANES) + bias[None]
        m_prev = m_sc[...]
        m_new = jnp.maximum(m_prev, jnp.max(jnp.max(st, axis=0), axis=0, keepdims=True))
        alpha = jnp.exp2(m_prev - m_new)
        pr = jnp.exp2(st - m_new)
        l_sc[...] = alpha * l_sc[...] + jnp.sum(jnp.sum(pr, axis=0), axis=0, keepdims=True)
        m_sc[...] = m_new
        pbuf[0:r, :] = pr.reshape(r, LANES)
        pv = pbuf[pl.ds(0, r // 2, stride=2), :] + pbuf[pl.ds(1, r // 2, stride=2), :]
        vt = v_rows.T.astype(BF16)
        acc_sc[...] = alpha * acc_sc[...] + jnp.dot(vt, pv.astype(BF16), preferred_element_type=F32)

    for kp_ref, vp_ref in zip(kp_refs, vp_refs):
        attend(kp_ref[...], vp_ref[...], bias_ref[...])

    @pl.when(p == pl.num_programs(1) - 1)
    def _():
        attend(kn_ref[...], vn_ref[...], biasn_ref[...])
        lam = _diff_lambda(lq1, lk1, lq2, lk2, lam_init)
        on = acc_sc[...] / l_sc[...]
        d = on - lam * pltpu.roll(on, LANES - n_new, 1)
        d = d * lax.rsqrt(jnp.mean(d * d, axis=0, keepdims=True) + SUBLN_EPS)
        o = d.T * subln_ref[...] * (1.0 - lam_init)
        for h in range(N_HEADS):
            o_ref[:, h * HEAD_W:(h + 1) * HEAD_W] = o[h * 2 * n_new:h * 2 * n_new + n_new]


def attn_sample(q, k_new, v_new, cache_k, cache_v, page_table, lams, subln, layer, lam_init, n_pp=8):
    b, n_new, d = q.shape
    n_pages = page_table.shape[1]
    n_phys = cache_k.shape[1]
    kh = 2 * N_HEADS
    nrow = N_HEADS * 2 * n_new
    n_slot = 16
    assert nrow <= LANES and n_new <= n_slot
    k_view = cache_k.reshape(cache_k.shape[0] * n_phys, PAGE_SIZE * kh, HEAD_DIM)
    v_view = cache_v.reshape(cache_v.shape[0] * n_phys, PAGE_SIZE * N_HEADS, HEAD_W)
    qrows = q.reshape(b, n_new, kh, HEAD_DIM).transpose(0, 2, 1, 3).reshape(b, nrow, HEAD_DIM)
    qrows = jnp.pad(qrows, ((0, 0), (0, LANES - nrow), (0, 0)))
    kn = jnp.pad(k_new, ((0, 0), (0, (n_slot - n_new) * kh), (0, 0)))
    vn = jnp.pad(v_new, ((0, 0), (0, n_slot - n_new), (0, 0))).reshape(b, n_slot * N_HEADS, HEAD_W)
    lane = jnp.arange(LANES)
    own = (jnp.arange(kh)[:, None] == (lane // n_new)[None, :]) | (lane >= nrow)[None, :]
    bias = jnp.where(own, 0.0, -jnp.inf).astype(F32)
    tokn = jnp.arange(n_slot)[:, None, None]
    seen = (tokn < n_new) & (tokn <= (lane % n_new)[None, None, :])
    bias_new = jnp.where((own[None] & seen) | (lane >= nrow)[None, None, :], 0.0, -jnp.inf)
    bias_new = bias_new.astype(F32).reshape(n_slot * kh, LANES)
    tok = pl.BlockSpec((None, n_new, d), lambda i, p, pt: (i, 0, 0))

    def per_seq(rows, width):
        return pl.BlockSpec((None, rows, width), lambda i, p, pt: (i, 0, 0))

    def page_spec(rows, width, k):
        return pl.BlockSpec((None, rows, width),
                            lambda i, p, pt: (layer * n_phys + pt[i * n_pages + p * n_pp + k], 0, 0))

    def const(rows, width):
        return pl.BlockSpec((rows, width), lambda i, p, pt: (0, 0))

    grid_spec = pltpu.PrefetchScalarGridSpec(
        num_scalar_prefetch=1,
        grid=(b, n_pages // n_pp),
        in_specs=[per_seq(LANES, HEAD_DIM), per_seq(n_slot * kh, HEAD_DIM), per_seq(n_slot * N_HEADS, HEAD_W)]
                 + [page_spec(PAGE_SIZE * kh, HEAD_DIM, k) for k in range(n_pp)]
                 + [page_spec(PAGE_SIZE * N_HEADS, HEAD_W, k) for k in range(n_pp)]
                 + [const(kh, LANES), const(n_slot * kh, LANES)]
                 + [const(1, HEAD_DIM)] * 4 + [const(1, HEAD_W)],
        out_specs=tok,
        scratch_shapes=[pltpu.VMEM((PAGE_SIZE * kh, LANES), F32), pltpu.VMEM((1, LANES), F32),
                        pltpu.VMEM((1, LANES), F32), pltpu.VMEM((HEAD_W, LANES), F32)])
    return pl.pallas_call(
        functools.partial(_attn_sample_kernel, n_new=n_new, n_pp=n_pp, lam_init=lam_init),
        grid_spec=grid_spec,
        out_shape=jax.ShapeDtypeStruct((b, n_new, d), F32),
        compiler_params=_params(("arbitrary", "arbitrary")),
        name="attn_sample",
    )(page_table.reshape(-1), qrows, kn, vn, *([k_view] * n_pp), *([v_view] * n_pp), bias, bias_new, *lams, subln)


def _router_kernel(xp_ref, xs_ref, g_ref, wh_ref, wl_ref, b_ref, tri_ref, xn_ref, ids_ref, gates_ref, cnt_ref,
                   run_sc, *, n_p, n_rows):
    i = pl.program_id(0)
    tm = xp_ref.shape[0]

    @pl.when(i == 0)
    def _():
        run_sc[...] = jnp.zeros(run_sc.shape, F32)

    x = jnp.where(i < n_p, xp_ref[...], xs_ref[...])
    xn = _rms(x, g_ref[...], RMS_EPS)
    xn_ref[...] = xn
    xh, xl = _split3(xn)
    d = functools.partial(jnp.dot, preferred_element_type=F32)
    logits = d(xh, wh_ref[...]) + d(xl, wh_ref[...]) + d(xh, wl_ref[...]) + b_ref[...]
    lane = lax.broadcasted_iota(jnp.int32, logits.shape, 1)
    ninf = -jnp.inf
    gl = jnp.where(lane < N_EXPERT_GROUPS, logits, ninf)
    gmax = jnp.max(gl, axis=-1, keepdims=True)
    g_val = 1.0 / jnp.sum(jnp.exp(gl - gmax), axis=-1, keepdims=True)
    g_idx = jnp.min(jnp.where(gl == gmax, lane, LANES), axis=-1, keepdims=True)
    lo = N_EXPERT_GROUPS + EXPERTS_PER_GROUP * g_idx
    el = jnp.where((lane >= lo) & (lane < lo + EXPERTS_PER_GROUP), logits, ninf)
    e1 = jnp.max(el, axis=-1, keepdims=True)
    i1 = jnp.min(jnp.where(el == e1, lane, LANES), axis=-1, keepdims=True)
    el2 = jnp.where(lane == i1, ninf, el)
    e2 = jnp.max(el2, axis=-1, keepdims=True)
    i2 = jnp.min(jnp.where(el2 == e2, lane, LANES), axis=-1, keepdims=True)
    r = jnp.exp(e2 - e1)
    w1 = g_val / (1.0 + r)
    w2 = g_val * r / (1.0 + r)
    chose = jnp.where(lane == i1, 1.0, jnp.where(lane == i2, 1.0, 0.0))
    real = (i * tm + lax.broadcasted_iota(jnp.int32, logits.shape, 0)) < n_rows
    chose = jnp.where(real, chose, 0.0)
    before = jnp.dot(tri_ref[...], chose.astype(BF16), preferred_element_type=F32) + run_sc[...]
    r1 = jnp.sum(jnp.where(lane == i1, before, 0.0), axis=-1, keepdims=True).astype(jnp.int32)
    r2 = jnp.sum(jnp.where(lane == i2, before, 0.0), axis=-1, keepdims=True).astype(jnp.int32)
    run_sc[...] = run_sc[...] + jnp.sum(chose, axis=0, keepdims=True)
    ids_ref[...] = jnp.where(lane == 0, i1 - N_EXPERT_GROUPS, jnp.where(lane == 1, i2 - N_EXPERT_GROUPS,
                             jnp.where(lane == 2, r1, jnp.where(lane == 3, r2, 0))))
    gates_ref[...] = jnp.where(lane == 0, w1, jnp.where(lane == 1, w2, 0.0))

    @pl.when(i == pl.num_programs(0) - 1)
    def _():
        cnt_ref[...] = run_sc[...]


def moe_route(xp, xs, g, w_hi, w_lo, bias, tm):
    (tp, d), ts = xp.shape, xs.shape[0]
    n_p, n_s = tp // tm, -(-ts // tm)
    xs = jnp.pad(xs, ((0, n_s * tm - ts), (0, 0)))
    t = tp + n_s * tm
    row = lambda i: (i, 0)
    const = lambda i: (0, 0)
    tri = jnp.tril(jnp.ones((tm, tm), BF16), -1)
    return pl.pallas_call(
        functools.partial(_router_kernel, n_p=n_p, n_rows=tp + ts),
        grid=(n_p + n_s,),
        in_specs=[pl.BlockSpec((tm, d), lambda i: (jnp.minimum(i, n_p - 1), 0)),
                  pl.BlockSpec((tm, d), lambda i: (jnp.maximum(i - n_p, 0), 0)),
                  pl.BlockSpec((1, d), const),
                  pl.BlockSpec((d, LANES), const), pl.BlockSpec((d, LANES), const),
                  pl.BlockSpec((1, LANES), const), pl.BlockSpec((tm, tm), const)],
        out_specs=[pl.BlockSpec((tm, d), row), pl.BlockSpec((tm, LANES), row), pl.BlockSpec((tm, LANES), row),
                   pl.BlockSpec((1, LANES), const)],
        out_shape=[jax.ShapeDtypeStruct((t, d), F32), jax.ShapeDtypeStruct((t, LANES), jnp.int32),
                   jax.ShapeDtypeStruct((t, LANES), F32), jax.ShapeDtypeStruct((1, LANES), F32)],
        scratch_shapes=[pltpu.VMEM((1, LANES), F32)],
        compiler_params=_params(("arbitrary",)),
        name="moe_route",
    )(xp, xs, g, w_hi, w_lo, bias, tri)


SUBLANES = 8
GATHER_UNROLL = 32


def _row_gather_start(idx_ref, base, n, src_hbm, dst, sem):
    tiles = GATHER_UNROLL // SUBLANES

    def body(g, c):
        for j in range(GATHER_UNROLL):
            idx = idx_ref[base + g * GATHER_UNROLL + j]
            pltpu.make_async_copy(src_hbm.at[pl.ds(idx, 1), :],
                                  dst.at[g * tiles + j // SUBLANES, pl.ds(j % SUBLANES, 1), :], sem).start()
        return c
    lax.fori_loop(0, n // GATHER_UNROLL, body, 0)


def _row_gather_wait(dst, sem):
    pltpu.make_async_copy(dst, dst, sem).wait()


W_CHUNK = 256
W_STAGES = 4


def _expert_kernel(bexp_ref, rtok_ref, nused_ref, nxt_ref, c0_ref, c1_ref, slot_ref,
                   xn_hbm, w13_hbm, w2_hbm, ys_ref, xbuf, sem, wb13, wb2, stage, wsem, *, tm, layer):
    rb = pl.program_id(0)
    nused = nused_ref[0]
    n13 = wb13.shape[1] // W_CHUNK
    nc = n13 + wb2.shape[1] // W_CHUNK

    def start_chunk(e, c):
        s = c % W_STAGES

        @pl.when(c < n13)
        def _():
            pltpu.make_async_copy(w13_hbm.at[layer, e, pl.ds(c * W_CHUNK, W_CHUNK), :], stage.at[s], wsem.at[s]).start()

        @pl.when(c >= n13)
        def _():
            pltpu.make_async_copy(w2_hbm.at[layer, e, pl.ds((c - n13) * W_CHUNK, W_CHUNK), :], stage.at[s],
                                  wsem.at[s]).start()

    def finish_chunk(c, dst):
        s = c % W_STAGES
        pltpu.make_async_copy(stage.at[s], stage.at[s], wsem.at[s]).wait()
        v = stage[s].astype(BF16)

        @pl.when(c < n13)
        def _():
            wb13[dst, pl.ds(pl.multiple_of(c * W_CHUNK, W_CHUNK), W_CHUNK), :] = v

        @pl.when(c >= n13)
        def _():
            wb2[dst, pl.ds(pl.multiple_of((c - n13) * W_CHUNK, W_CHUNK), W_CHUNK), :] = v

    def stream(e, lo, hi, dst):
        @pl.when((lo == 0) & (hi > 0))
        def _():
            for c in range(W_STAGES):
                start_chunk(e, jnp.int32(c))

        def body(c, carry):
            finish_chunk(c, dst)

            @pl.when(c + W_STAGES < nc)
            def _():
                start_chunk(e, c + W_STAGES)
            return carry

        lax.fori_loop(lo, hi, body, 0)

    @pl.when(rb == 0)
    def _():
        stream(bexp_ref[0], 0, nc, 0)
        _row_gather_start(rtok_ref, 0, tm, xn_hbm, xbuf.at[0], sem.at[0])

    @pl.when(rb + 1 < nused)
    def _():
        nxt = (rb + 1) % 2
        _row_gather_start(rtok_ref, (rb + 1) * tm, tm, xn_hbm, xbuf.at[nxt], sem.at[nxt])

    @pl.when(rb < nused)
    def _():
        cur = slot_ref[rb]
        slot = rb % 2
        _row_gather_wait(xbuf.at[slot], sem.at[slot])
        x = xbuf[slot].reshape(tm, xbuf.shape[-1]).astype(BF16)
        h = jnp.dot(x, wb13[cur], preferred_element_type=F32)
        ga, up = h[:, :EXPERT_FF], h[:, EXPERT_FF:]
        a = ga * (1.0 / (1.0 + jnp.exp(-ga))) * up
        ys_ref[...] = jnp.dot(a.astype(BF16), wb2[cur], preferred_element_type=F32)
        stream(nxt_ref[rb], c0_ref[rb], c1_ref[rb], 1 - cur)

    @pl.when(rb >= nused)
    def _():
        ys_ref[...] = jnp.zeros(ys_ref.shape, F32)


def moe_experts(xn, w13, w2, layer, block_exp, row_tok, n_used, sched, tm):
    t, d = xn.shape
    n_blocks = block_exp.shape[0]
    anyspace = pl.BlockSpec(memory_space=pl.ANY)
    grid_spec = pltpu.PrefetchScalarGridSpec(
        num_scalar_prefetch=7,
        grid=(n_blocks,),
        in_specs=[anyspace, anyspace, anyspace],
        out_specs=pl.BlockSpec((tm, d), lambda rb, *_: (rb, 0)),
        scratch_shapes=[pltpu.VMEM((2, tm // SUBLANES, SUBLANES, d), F32), pltpu.SemaphoreType.DMA((2,)),
                        pltpu.VMEM((2, d, 2 * EXPERT_FF), BF16), pltpu.VMEM((2, EXPERT_FF, d), BF16),
                        pltpu.VMEM((W_STAGES, W_CHUNK, d), F32), pltpu.SemaphoreType.DMA((W_STAGES,))])
    return pl.pallas_call(
        functools.partial(_expert_kernel, tm=tm, layer=layer),
        grid_spec=grid_spec,
        out_shape=jax.ShapeDtypeStruct((n_blocks * tm, d), F32),
        compiler_params=_params(("arbitrary",)),
        name="moe_experts",
    )(block_exp, row_tok, n_used, *sched, xn, w13, w2)


def _weight_schedule(padded, block_exp, tm, n_chunks):
    e_idx = jnp.arange(N_EXPERTS, dtype=jnp.int32)
    nonempty = padded > 0
    run = jnp.cumsum(nonempty.astype(jnp.int32)) - 1
    later = jnp.where(nonempty, e_idx, N_EXPERTS)
    nxt = jnp.concatenate([lax.cummin(later[::-1])[::-1][1:], jnp.full((1,), N_EXPERTS, jnp.int32)])
    has_next = nxt < N_EXPERTS
    nblk = padded // tm
    first = (jnp.cumsum(padded) - padded) // tm
    e = block_exp
    j = jnp.arange(block_exp.shape[0], dtype=jnp.int32) - first[e]
    n = jnp.maximum(nblk[e], 1)
    live = has_next[e] & (j >= 0) & (j < nblk[e])
    lo = jnp.where(live, j * n_chunks // n, 0)
    hi = jnp.where(live, (j + 1) * n_chunks // n, 0)
    return (jnp.where(has_next[e], nxt[e], e).astype(jnp.int32), lo.astype(jnp.int32), hi.astype(jnp.int32),
            (run[e] % 2).astype(jnp.int32))


def _combine_kernel(pos_ref, ys_hbm, x_ref, gates_ref, gout_ref, o_ref, ybuf, sem, *, tm, norm_out):
    i = pl.program_id(0)
    n = pl.num_programs(0)

    def start(tile, slot):
        _row_gather_start(pos_ref, tile * 2 * tm, 2 * tm, ys_hbm, ybuf.at[slot], sem.at[slot])

    @pl.when(i == 0)
    def _():
        start(0, 0)

    @pl.when(i + 1 < n)
    def _():
        start(i + 1, (i + 1) % 2)

    slot = i % 2
    _row_gather_wait(ybuf.at[slot], sem.at[slot])
    g = gates_ref[...]
    nt, d = tm // SUBLANES, ybuf.shape[-1]
    y0 = ybuf[slot, :nt].reshape(tm, d)
    y1 = ybuf[slot, nt:].reshape(tm, d)
    out = x_ref[...] + (g[:, 0:1] * y0 + g[:, 1:2] * y1)
    o_ref[...] = _rms(out, gout_ref[...], RMS_EPS) if norm_out else out


def moe_combine(ys, x, gates, pos, tm, g_out, norm_out):
    t, d = x.shape
    pos_tiles = pos.reshape(t // tm, tm, 2).transpose(0, 2, 1).reshape(-1)
    grid_spec = pltpu.PrefetchScalarGridSpec(
        num_scalar_prefetch=1,
        grid=(t // tm,),
        in_specs=[pl.BlockSpec(memory_space=pl.ANY),
                  pl.BlockSpec((tm, d), lambda i, ps: (i, 0)),
                  pl.BlockSpec((tm, LANES), lambda i, ps: (i, 0)),
                  pl.BlockSpec((1, d), lambda i, ps: (0, 0))],
        out_specs=pl.BlockSpec((tm, d), lambda i, ps: (i, 0)),
        scratch_shapes=[pltpu.VMEM((2, 2 * tm // SUBLANES, SUBLANES, d), F32), pltpu.SemaphoreType.DMA((2,))])
    return pl.pallas_call(
        functools.partial(_combine_kernel, tm=tm, norm_out=norm_out),
        grid_spec=grid_spec,
        out_shape=jax.ShapeDtypeStruct((t, d), F32),
        compiler_params=_params(("arbitrary",)),
        name="moe_combine",
    )(pos_tiles, ys, x, gates, g_out)


def hier_moe(xp, xs, g, w_group, b_group, w_expert, b_expert, w13, w2, layer, g_out, norm_out,
             tm_blk=256, tm_tok=128, tm_route=512):
    tp, d = xp.shape
    pad = LANES - N_EXPERT_GROUPS - N_EXPERTS
    w_cat = jnp.concatenate([w_group, w_expert, jnp.zeros((d, pad), F32)], axis=1)
    b_cat = jnp.concatenate([b_group, b_expert, jnp.zeros((pad,), F32)])[None]
    w_hi = w_cat.astype(BF16)
    w_lo = (w_cat - w_hi.astype(F32)).astype(BF16)
    xn, ids, gates, cnt = moe_route(xp, xs, g, w_hi, w_lo, b_cat, tm_route)
    t = tp + xs.shape[0]
    ids = ids[:t]
    n_assign = 2 * t
    counts = cnt[0, N_EXPERT_GROUPS:N_EXPERT_GROUPS + N_EXPERTS].astype(jnp.int32)
    padded = ((counts + tm_blk - 1) // tm_blk) * tm_blk
    pad_end = jnp.cumsum(padded)
    pad_start = pad_end - padded
    pos = pad_start[ids[:, :2]] + ids[:, 2:4]
    n_blocks = -(-n_assign // tm_blk) + N_EXPERTS
    flat_t = jnp.arange(n_assign, dtype=jnp.int32) // 2
    row_tok = jnp.zeros((n_blocks * tm_blk,), jnp.int32).at[pos.reshape(n_assign)].set(flat_t)
    blk_start = jnp.arange(n_blocks, dtype=jnp.int32) * tm_blk
    block_exp = jnp.minimum(jnp.sum((pad_end[None, :] <= blk_start[:, None]).astype(jnp.int32), axis=1), N_EXPERTS - 1)
    n_used = (pad_end[-1:] // tm_blk).astype(jnp.int32)
    n_wchunks = (w13.shape[2] + w2.shape[2]) // W_CHUNK
    sched = _weight_schedule(padded, block_exp, tm_blk, n_wchunks)
    ys = moe_experts(xn, w13, w2, layer, block_exp, row_tok, n_used, sched, tm_blk)
    return (moe_combine(ys, xp, gates[:tp], pos[:tp], tm_tok, g_out, norm_out),
            moe_combine(ys, xs, gates[tp:t], pos[tp:], min(tm_tok, xs.shape[0]), g_out, norm_out))


def _cmul(ar, ai, br, bi):
    return ar * br - ai * bi, ar * bi + ai * br


def _ssm_prep_kernel(are_ref, aim_ref, ldt_ref, bre_ref, bim_ref, cre_ref, cim_ref,
                     m_ref, bc_ref, cc_ref, a16_ref, a4_ref, *, n_new):
    p = SSM_STATE
    nc = SSM_CHUNK
    a_re, a_im = are_ref[...], aim_ref[...]
    dt = jnp.exp(ldt_ref[...])
    mag = jnp.exp(dt * a_re)
    ab_re = mag * jnp.cos(dt * a_im)
    ab_im = mag * jnp.sin(dt * a_im)
    den = a_re * a_re + a_im * a_im
    nr = ab_re - 1.0
    fr = (nr * a_re + ab_im * a_im) / den
    fi = (ab_im * a_re - nr * a_im) / den
    bt_re, bt_im = bre_ref[...], bim_ref[...]
    bb_re = fr * bt_re - fi * bt_im
    bb_im = fr * bt_im + fi * bt_re
    c_re, c_im = cre_ref[...], cim_ref[...]
    pw = [(jnp.ones_like(ab_re), jnp.zeros_like(ab_re))]
    for _ in range(nc):
        pw.append(_cmul(pw[-1][0], pw[-1][1], ab_re, ab_im))
    wst = []
    for tau in range(nc + 1):
        wr, wi = _cmul(c_re, c_im, pw[tau][0], pw[tau][1])
        wst.append(jnp.concatenate([wr, -wi], axis=1))
    wstack = jnp.concatenate(wst[:nc], axis=0)
    bs_t = jnp.concatenate([bb_re, bb_im], axis=1)
    q = _dot3(bs_t, wstack, _NT)
    qpad = jnp.concatenate([jnp.zeros_like(q), q], axis=1)
    w = nc * SSM_GROUP_CH
    rows = []
    for s in range(nc):
        off = (nc - s) * SSM_GROUP_CH
        rows.append(qpad[:, off:off + w])
    m_ref[...] = jnp.concatenate(rows, axis=0).astype(m_ref.dtype)
    bc = []
    for s in range(nc):
        xr, xi = _cmul(bb_re, bb_im, pw[nc - 1 - s][0], pw[nc - 1 - s][1])
        bc.append(jnp.concatenate([xr, xi], axis=1))
    bc_ref[...] = jnp.concatenate(bc, axis=0).astype(bc_ref.dtype)
    cc_ref[...] = jnp.concatenate(wst[1:], axis=0).astype(cc_ref.dtype)
    a16_ref[...] = jnp.concatenate([pw[nc][0], pw[nc][1]], axis=1)
    a4_ref[...] = jnp.concatenate([pw[n_new][0], pw[n_new][1]], axis=1)


def ssm_prep(a_re, a_im, log_dt, b_re, b_im, c_re, c_im, n_new):
    g, p = a_re.shape
    gc = SSM_GROUP_CH
    w = SSM_CHUNK * gc
    vec = pl.BlockSpec((None, 1, p), lambda i: (i, 0, 0))
    mat = pl.BlockSpec((None, gc, p), lambda i: (i, 0, 0))
    bt_re = b_re.transpose(0, 2, 1)
    bt_im = b_im.transpose(0, 2, 1)
    return pl.pallas_call(
        functools.partial(_ssm_prep_kernel, n_new=n_new),
        grid=(g,),
        in_specs=[vec, vec, pl.BlockSpec((None, 1, 1), lambda i: (i, 0, 0)), mat, mat, mat, mat],
        out_specs=[pl.BlockSpec((None, w, w), lambda i: (i, 0, 0)),
                   pl.BlockSpec((None, w, 2 * p), lambda i: (i, 0, 0)),
                   pl.BlockSpec((None, w, 2 * p), lambda i: (i, 0, 0)),
                   pl.BlockSpec((None, 1, 2 * p), lambda i: (i, 0, 0)),
                   pl.BlockSpec((None, 1, 2 * p), lambda i: (i, 0, 0))],
        out_shape=[jax.ShapeDtypeStruct((g, w, w), BF16),
                   jax.ShapeDtypeStruct((g, w, 2 * p), BF16),
                   jax.ShapeDtypeStruct((g, w, 2 * p), BF16),
                   jax.ShapeDtypeStruct((g, 1, 2 * p), F32),
                   jax.ShapeDtypeStruct((g, 1, 2 * p), F32)],
        compiler_params=_params(("arbitrary",)),
        name="ssm_prep",
    )(a_re[:, None], a_im[:, None], log_dt[:, None, None], bt_re, bt_im, c_re, c_im)


GROUPS_PER_VREG = LANES // SSM_GROUP_CH
PERM_BATCH = 8


def _chunk_perm():
    n = GROUPS_PER_VREG
    idx = np.arange(n * LANES)
    a, b, c = idx // LANES, (idx // SSM_GROUP_CH) % n, idx % SSM_GROUP_CH
    m = np.zeros((n * LANES, n * LANES), np.float32)
    m[idx, (b * n + a) * SSM_GROUP_CH + c] = 1.0
    return jnp.asarray(m, BF16)


def _ssm_in_kernel(*refs):
    nj = (len(refs) - 3) // 2
    x_refs, g_refs = refs[:nj], refs[nj:2 * nj]
    p_ref, u_ref, inv_sc = refs[2 * nj], refs[2 * nj + 1], refs[2 * nj + 2]
    tm = x_refs[0].shape[0]
    nc, n = tm // SSM_CHUNK, GROUPS_PER_VREG
    ssq = jnp.zeros((tm, LANES), F32)
    for x_ref in x_refs:
        xj = x_ref[...]
        ssq = ssq + xj * xj
    inv = lax.rsqrt(jnp.sum(ssq, axis=-1, keepdims=True) * (1.0 / (nj * LANES)) + RMS_EPS)
    inv_sc[...] = jnp.broadcast_to(inv, (tm, LANES))

    def piece(j, s):
        rows = pl.ds(s, nc, stride=SSM_CHUNK)
        return x_refs[j][rows, :] * inv_sc[rows, :] * g_refs[j][...]

    pairs = [(j, h) for j in range(nj) for h in range(SSM_CHUNK // n)]
    for p0 in range(0, len(pairs), PERM_BATCH):
        batch = pairs[p0:p0 + PERM_BATCH]
        lhs = jnp.concatenate(
            [jnp.concatenate([piece(j, h * n + sl) for sl in range(n)], axis=1).astype(BF16) for j, h in batch],
            axis=0)
        r = jnp.dot(lhs, p_ref[...], preferred_element_type=F32)
        for k, (j, h) in enumerate(batch):
            for gl in range(n):
                u_ref[j * n + gl, :, h * LANES:(h + 1) * LANES] = (
                    r[k * nc:(k + 1) * nc, gl * LANES:(gl + 1) * LANES].astype(BF16))


def ssm_chunk_in(x, g, perm, tm):
    t, d = x.shape
    ng, w = d // SSM_GROUP_CH, SSM_CHUNK * SSM_GROUP_CH
    nc, nj = tm // SSM_CHUNK, d // LANES
    return pl.pallas_call(
        _ssm_in_kernel,
        grid=(t // tm,),
        in_specs=[pl.BlockSpec((tm, LANES), lambda i, j=j: (i, j)) for j in range(nj)]
                 + [pl.BlockSpec((1, LANES), lambda i, j=j: (0, j)) for j in range(nj)]
                 + [pl.BlockSpec(perm.shape, lambda i: (0, 0))],
        out_specs=pl.BlockSpec((ng, nc, w), lambda i: (0, i, 0)),
        out_shape=jax.ShapeDtypeStruct((ng, t // SSM_CHUNK, w), BF16),
        scratch_shapes=[pltpu.VMEM((tm, LANES), F32)],
        compiler_params=_params(("arbitrary",)),
        name="ssm_chunk_in",
    )(*([x] * nj), *([g] * nj), perm)


def _ssm_out_kernel(z_ref, p_ref, o_ref):
    nj, tm, _ = o_ref.shape
    nc, n = tm // SSM_CHUNK, GROUPS_PER_VREG
    pairs = [(j, h) for j in range(nj) for h in range(SSM_CHUNK // n)]
    for p0 in range(0, len(pairs), PERM_BATCH):
        batch = pairs[p0:p0 + PERM_BATCH]
        lhs = jnp.concatenate(
            [jnp.concatenate([z_ref[j * n + gl, :, h * LANES:(h + 1) * LANES] for gl in range(n)], axis=1)
             for j, h in batch], axis=0)
        r = jnp.dot(lhs, p_ref[...], preferred_element_type=F32)
        for k, (j, h) in enumerate(batch):
            for tl in range(n):
                o_ref[j, pl.ds(h * n + tl, nc, stride=SSM_CHUNK), :] = (
                    r[k * nc:(k + 1) * nc, tl * LANES:(tl + 1) * LANES])


def ssm_chunk_out(z, perm, tm, d):
    ng, n_c, w = z.shape
    nc, nj = tm // SSM_CHUNK, d // LANES
    return pl.pallas_call(
        _ssm_out_kernel,
        grid=(n_c // nc,),
        in_specs=[pl.BlockSpec((ng, nc, w), lambda i: (0, i, 0)), pl.BlockSpec(perm.shape, lambda i: (0, 0))],
        out_specs=pl.BlockSpec((nj, tm, LANES), lambda i: (0, i, 0)),
        out_shape=jax.ShapeDtypeStruct((nj, n_c * SSM_CHUNK, LANES), F32),
        compiler_params=_params(("arbitrary",)),
        name="ssm_chunk_out",
    )(z, perm)


def _ssm_x_kernel(u_ref, bc_ref, x_ref, *, k):
    for gi in range(u_ref.shape[0]):
        bc = bc_ref[gi]
        x_ref[gi] = jnp.dot(u_ref[gi, :, :k], bc[bc.shape[0] - k:], preferred_element_type=F32)


def _ssm_scan_kernel(xp_ref, xs_ref, h0_ref, a16_ref, a4_ref, hs_ref, hfin_ref, hsfin_ref, *, n_chunks, batch):
    p = SSM_STATE

    def coef(a):
        ar, ai = a[:, :, :p], a[:, :, p:]
        return jnp.concatenate([ar, ar], axis=-1), jnp.concatenate([-ai, ai], axis=-1)

    def step(h, c1, c2):
        return c1 * h + c2 * pltpu.roll(h, p, 2)

    c1, c2 = coef(a16_ref[...])

    def body(c, h):
        rows = pl.ds(c, batch, stride=n_chunks)
        hs_ref[:, rows, :] = h
        return step(h, c1, c2) + xp_ref[:, rows, :]

    h0 = jnp.zeros((xp_ref.shape[0], batch, 2 * p), F32)
    hfin_ref[...] = lax.fori_loop(0, n_chunks, body, h0)
    d1, d2 = coef(a4_ref[...])
    hsfin_ref[...] = step(h0_ref[...], d1, d2) + xs_ref[...]


def _ssm_y_kernel(u_ref, hs_ref, m_ref, cc_ref, d_ref, z_ref):
    c = math.sqrt(2.0 / math.pi)
    for gi in range(u_ref.shape[0]):
        u = u_ref[gi]
        y = jnp.dot(u, m_ref[gi], preferred_element_type=F32)
        y = y + lax.dot_general(hs_ref[gi].astype(BF16), cc_ref[gi], (_NT, ((), ())), preferred_element_type=F32)
        y = y + d_ref[gi] * u.astype(F32)
        z = 0.5 * y * (1.0 + jnp.tanh(c * (y + 0.044715 * (y * y * y))))
        z_ref[gi] = z.astype(z_ref.dtype)


def ssm_mix(up, us, h0_g, ops, d_tile, n_chunks, batch, n_new, g_blk=8, g_scan=16):
    m_g, bc_g, cc_g, a16, a4 = ops
    g, _, w = up.shape
    p2 = 2 * SSM_STATE
    n_s = us.shape[1]

    def blk(gb, *shape):
        return pl.BlockSpec((gb,) + shape, lambda i: (i,) + (0,) * len(shape))

    def chunk_state(u, k):
        r = u.shape[1]
        return pl.pallas_call(
            functools.partial(_ssm_x_kernel, k=k),
            grid=(g // g_blk,),
            in_specs=[blk(g_blk, r, w), blk(g_blk, w, p2)],
            out_specs=blk(g_blk, r, p2),
            out_shape=jax.ShapeDtypeStruct((g, r, p2), F32),
            compiler_params=_params(("arbitrary",)),
            name="ssm_chunk_state",
        )(u, bc_g)

    def output(u, hs):
        r = u.shape[1]
        return pl.pallas_call(
            _ssm_y_kernel,
            grid=(g // g_blk,),
            in_specs=[blk(g_blk, r, w), blk(g_blk, r, p2), blk(g_blk, w, w), blk(g_blk, w, p2), blk(g_blk, 1, w)],
            out_specs=blk(g_blk, r, w),
            out_shape=jax.ShapeDtypeStruct((g, r, w), BF16),
            compiler_params=_params(("arbitrary",)),
            name="ssm_output",
        )(u, hs, m_g, cc_g, d_tile)

    xp = chunk_state(up, w)
    xs = chunk_state(us, n_new * SSM_GROUP_CH)
    r = up.shape[1]
    hs, hfin, hsfin = pl.pallas_call(
        functools.partial(_ssm_scan_kernel, n_chunks=n_chunks, batch=batch),
        grid=(g // g_scan,),
        in_specs=[blk(g_scan, r, p2), blk(g_scan, n_s, p2), blk(g_scan, n_s, p2), blk(g_scan, 1, p2),
                  blk(g_scan, 1, p2)],
        out_specs=[blk(g_scan, r, p2), blk(g_scan, batch, p2), blk(g_scan, n_s, p2)],
        out_shape=[jax.ShapeDtypeStruct((g, r, p2), F32),
                   jax.ShapeDtypeStruct((g, batch, p2), F32),
                   jax.ShapeDtypeStruct((g, n_s, p2), F32)],
        compiler_params=_params(("arbitrary",)),
        name="ssm_scan",
    )(xp, xs, h0_g, a16, a4)
    return output(up, hs), output(us, h0_g), hfin, hsfin


def kernel(x_prompt, x_sample, cache_k, cache_v, state_ssm_re, state_ssm_im, page_table, norm_mix, norm_ffn, norm_final, attn_w_qkv, attn_lambda_q1, attn_lambda_k1, attn_lambda_q2, attn_lambda_k2, attn_subln, attn_w_o, ssm_A_re, ssm_A_im, ssm_log_dt, ssm_B_re, ssm_B_im, ssm_C_re, ssm_C_im, ssm_D, ssm_glu_w, ssm_glu_b, moe_w_group, moe_b_group, moe_w_expert, moe_b_expert, moe_w13, moe_w2):
    batch, seq, d = x_prompt.shape
    dec_b, dec_s, _ = x_sample.shape
    depth = norm_mix.shape[0]
    tp, ts = batch * seq, dec_b * dec_s
    xp = x_prompt.reshape(tp, d)
    xs = x_sample.reshape(ts, d)
    g, pst, gc = ssm_B_re.shape[1:]
    n_chunks = seq // SSM_CHUNK
    k_p, v_p, k_s, v_s = [], [], [], []
    hr_p, hi_p, hr_s, hi_s = [], [], [], []
    for i in range(depth):
        gm = norm_mix[i][None]
        if i % 2 == 0:
            a = i // 2
            lam_init = 0.8 - 0.6 * math.exp(-0.3 * i)
            lams = (attn_lambda_q1[a][None], attn_lambda_k1[a][None], attn_lambda_q2[a][None], attn_lambda_k2[a][None])
            subln = attn_subln[a][None]
            w_qkv = attn_w_qkv[a].astype(BF16)
            w_o = attn_w_o[a].astype(BF16)
            qp, khp, kp, vp = qkv_proj(xp, gm, w_qkv, 1024, 512)
            qs, _, ks, vs = qkv_proj(xs, gm, w_qkv, ts, 512)
            op = attn_prompt(qp, khp, vp, lams, subln, batch, seq, lam_init)
            os_ = attn_sample(qs.reshape(dec_b, dec_s, d), ks.reshape(dec_b, dec_s * 2 * N_HEADS, HEAD_DIM),
                              vs.reshape(dec_b, dec_s, d), cache_k, cache_v, page_table, lams, subln, a, lam_init)
            xp = matmul_residual(op, w_o, xp, 1024, 512)
            xs = matmul_residual(os_.reshape(ts, d), w_o, xs, ts, 512)
            k_p.append(kp.reshape(batch, seq, N_HEADS, 2, HEAD_DIM))
            v_p.append(vp.reshape(batch, seq, N_HEADS, HEAD_W))
            k_s.append(ks.reshape(dec_b, dec_s, N_HEADS, 2, HEAD_DIM))
            v_s.append(vs.reshape(dec_b, dec_s, N_HEADS, HEAD_W))
        else:
            s = i // 2
            ops = ssm_prep(ssm_A_re[s], ssm_A_im[s], ssm_log_dt[s], ssm_B_re[s], ssm_B_im[s],
                           ssm_C_re[s], ssm_C_im[s], dec_s)
            perm = _chunk_perm()
            up = ssm_chunk_in(xp, gm, perm, 1024)
            hs = rms_norm(xs, gm, ts)
            us = hs.reshape(dec_b, dec_s, g, gc).transpose(2, 0, 1, 3).reshape(g, dec_b, dec_s * gc)
            us = jnp.pad(us, ((0, 0), (0, 0), (0, (SSM_CHUNK - dec_s) * gc))).astype(BF16)
            h0_g = jnp.concatenate([state_ssm_re[s], state_ssm_im[s]], axis=-1).transpose(1, 0, 2)
            d_tile = jnp.tile(ssm_D[s], (1, SSM_CHUNK))[:, None]
            zp_g, zs_g, hfin, hsfin = ssm_mix(up, us, h0_g, ops, d_tile, n_chunks, batch, dec_s)
            zp = ssm_chunk_out(zp_g, perm, 1024, d)
            zs = zs_g[:, :, :dec_s * gc].reshape(g, dec_b, dec_s, gc).transpose(1, 2, 0, 3).reshape(ts, d)
            zs = zs.astype(F32).reshape(ts, d // LANES, LANES).transpose(1, 0, 2)
            glu_w = ssm_glu_w[s].astype(BF16)
            glu_b = ssm_glu_b[s][None]
            xp = glu_residual(zp, glu_w, glu_b, xp, 1024, 512)
            xs = glu_residual(zs, glu_w, glu_b, xs, ts, 512)
            hfin = hfin.transpose(1, 0, 2)
            hsfin = hsfin.transpose(1, 0, 2)
            hr_p.append(hfin[..., :pst])
            hi_p.append(hfin[..., pst:])
            hr_s.append(hsfin[..., :pst])
            hi_s.append(hsfin[..., pst:])
        xp, xs = hier_moe(xp, xs, norm_ffn[i][None], moe_w_group[i], moe_b_group[i], moe_w_expert[i],
                          moe_b_expert[i], moe_w13, moe_w2, i, norm_final[None], i == depth - 1)
    y_prompt = xp.reshape(batch, seq, d)
    y_sample = xs.reshape(dec_b, dec_s, d)
    return (y_prompt, y_sample, jnp.stack(k_p), jnp.stack(v_p), jnp.stack(k_s), jnp.stack(v_s),
            jnp.stack(hr_p), jnp.stack(hi_p), jnp.stack(hr_s), jnp.stack(hi_s))
```
